```python
import math
import jax
import jax.numpy as jnp
from jax import lax
import numpy as np

D_MODEL = 1024
BATCH = 16
SEQ = 4096
DEPTH = 2

GRID_W = 64
CTX_LEN = 256
EPS = 1e-6

SSD_HEADS = 6
SSD_HEAD_DIM = 64
SSD_INNER = SSD_HEADS * SSD_HEAD_DIM
SSD_GROUPS = 2
SSD_STATE = 128
SSD_CONV = 4
SSD_CHUNK = 128
XBC_DIM = SSD_INNER + 2 * SSD_GROUPS * SSD_STATE

MLA_HEADS = 6
Q_LORA = 256
KV_LORA = 256
QK_NOPE = 64
QK_ROPE = 32
V_HEAD = 64
QK_DIM = QK_NOPE + QK_ROPE
MLA_OUT = MLA_HEADS * V_HEAD
ROPE_THETA = 10000.0
ROPE_PAIRS = QK_ROPE // 4
Q_BLOCK = 128

POOL_WINDOWS = (2, 4, 8, 16)
POOL_GROUPS = len(POOL_WINDOWS)
POOL_GROUP_DIM = 64
POOL_DIM = POOL_GROUPS * POOL_GROUP_DIM

MIX_DIM = SSD_INNER + MLA_OUT + POOL_DIM
D_FF = 4 * D_MODEL

OFF_Z = 0
OFF_XBC = OFF_Z + SSD_INNER
OFF_DT = OFF_XBC + XBC_DIM
OFF_QA = OFF_DT + 2 * SSD_HEADS
OFF_KVA = OFF_QA + Q_LORA
OFF_KROPE = OFF_KVA + KV_LORA
OFF_POOL = OFF_KROPE + QK_ROPE
IN_COLS = OFF_POOL + POOL_DIM

kernel_name = "hybrid_ssd_mla_pool_diffusion_block"


def rmsnorm(u, w):
    uf = u.astype(jnp.float32)
    y = uf * lax.rsqrt(jnp.mean(uf * uf, axis=-1, keepdims=True) + EPS)
    return (y * w.astype(jnp.float32)).astype(u.dtype)


def adaln(cond, mod_w, mod_b):
    m = jax.nn.silu(cond) @ mod_w + mod_b
    return jnp.split(m[..., None, :], 6, axis=-1)


def modulate(h, shift, scale):
    return h * (1.0 + scale) + shift


def squared_relu_mlp(h, w1, w2):
    return jnp.square(jax.nn.relu(h @ w1)) @ w2


def axial_rope_tables(n):
    rows = n // GRID_W
    row = jnp.repeat(jnp.arange(rows, dtype=jnp.float32), GRID_W)
    col = jnp.tile(jnp.arange(GRID_W, dtype=jnp.float32), rows)
    inv_freq = ROPE_THETA ** (-jnp.arange(ROPE_PAIRS, dtype=jnp.float32) / ROPE_PAIRS)
    ang = jnp.stack([row[:, None] * inv_freq, col[:, None] * inv_freq], axis=1)
    return jnp.cos(ang), jnp.sin(ang)


def apply_axial_rope(u, cos, sin):
    shp = u.shape
    ur = u.astype(jnp.float32).reshape(shp[:-1] + (2, 2, ROPE_PAIRS))
    u1, u2 = ur[..., 0, :], ur[..., 1, :]
    bshape = (shp[1],) + (1,) * (u.ndim - 3) + (2, ROPE_PAIRS)
    cos = cos.reshape(bshape)
    sin = sin.reshape(bshape)
    out = jnp.stack([u1 * cos - u2 * sin, u2 * cos + u1 * sin], axis=-2)
    return out.reshape(shp).astype(u.dtype)


def depthwise_conv_centred(u, w, b):
    k = w.shape[0]
    out = lax.conv_general_dilated(u, w[:, None, :].astype(u.dtype), (1,), [((k - 1) // 2, k // 2)],
                                   dimension_numbers=('NWC', 'WIO', 'NWC'),
                                   feature_group_count=u.shape[-1])
    return out + b


def ssd_inputs(proj, conv_w, conv_b, dt_bias):
    bsz, n = proj.shape[:2]
    z = proj[..., OFF_Z:OFF_XBC]
    xbc = jax.nn.silu(depthwise_conv_centred(proj[..., OFF_XBC:OFF_DT], conv_w, conv_b))
    xs = xbc[..., :SSD_INNER].reshape(bsz, n, SSD_HEADS, SSD_HEAD_DIM)
    bm = xbc[..., SSD_INNER:SSD_INNER + SSD_GROUPS * SSD_STATE].reshape(bsz, n, SSD_GROUPS, SSD_STATE)
    cm = xbc[..., SSD_INNER + SSD_GROUPS * SSD_STATE:].reshape(bsz, n, SSD_GROUPS, SSD_STATE)
    dt_raw = proj[..., OFF_DT:OFF_QA].astype(jnp.float32).reshape(bsz, n, 2, SSD_HEADS)
    dt = jax.nn.softplus(dt_raw + dt_bias.astype(jnp.float32))
    return z, xs, bm, cm, dt


def segsum_exp(a):
    l = a.shape[-1]
    cs = jnp.cumsum(a, axis=-1)
    diff = cs[..., :, None] - cs[..., None, :]
    mask = jnp.tril(jnp.ones((l, l), dtype=bool))
    return jnp.exp(jnp.where(mask, diff, -jnp.inf))


def ssd_chunked_scan(xs, dt, a, b_mat, c_mat, h0):
    f32 = jnp.float32
    bsz, n = xs.shape[:2]
    nc = n // SSD_CHUNK
    r = SSD_HEADS // SSD_GROUPS
    dt = dt.astype(f32)
    x = (xs.astype(f32) * dt[..., None]).reshape(bsz, nc, SSD_CHUNK, SSD_GROUPS, r, SSD_HEAD_DIM)
    adt = jnp.moveaxis((dt * a.astype(f32)).reshape(bsz, nc, SSD_CHUNK, SSD_GROUPS, r), 2, -1)
    bm = b_mat.astype(f32).reshape(bsz, nc, SSD_CHUNK, SSD_GROUPS, SSD_STATE)
    cm = c_mat.astype(f32).reshape(bsz, nc, SSD_CHUNK, SSD_GROUPS, SSD_STATE)
    a_cs = jnp.cumsum(adt, axis=-1)
    l_mat = segsum_exp(adt)
    cb = jnp.einsum('bclgn,bcsgn->bcgls', cm, bm)
    y_diag = jnp.einsum('bcgrls,bcsgrp->bclgrp', cb[:, :, :, None] * l_mat, x)
    decay_to_end = jnp.exp(a_cs[..., -1:] - a_cs)
    chunk_states = jnp.einsum('bclgn,bcgrl,bclgrp->bcgrpn', bm, decay_to_end, x)
    chunk_decay = jnp.exp(a_cs[..., -1])

    def carry_step(h, inp):
        s_c, d_c = inp
        return h * d_c[..., None, None] + s_c, h

    h_init = h0.astype(f32).reshape(bsz, SSD_GROUPS, r, SSD_HEAD_DIM, SSD_STATE)
    h_last, h_in = lax.scan(carry_step, h_init,
                            (jnp.moveaxis(chunk_states, 1, 0), jnp.moveaxis(chunk_decay, 1, 0)))
    h_in = jnp.moveaxis(h_in, 0, 1)
    y_off = jnp.einsum('bclgn,bcgrpn,bcgrl->bclgrp', cm, h_in, jnp.exp(a_cs))
    y = (y_diag + y_off).reshape(bsz, n, SSD_HEADS, SSD_HEAD_DIM)
    return y, h_last.reshape(bsz, SSD_HEADS, SSD_HEAD_DIM, SSD_STATE)


def ssd_bidirectional(xs, bm, cm, dt, a, h0_fwd, h0_bwd):
    flip = lambda t: jnp.flip(t, axis=1)
    y_f, h_f = ssd_chunked_scan(xs, dt[:, :, 0], a[0], bm, cm, h0_fwd)
    y_b, h_b = ssd_chunked_scan(flip(xs), flip(dt[:, :, 1]), a[1], flip(bm), flip(cm), h0_bwd)
    return y_f + flip(y_b), h_f, h_b


def ssd_output(y, xs, z, d_skip, norm_w):
    bsz, n = y.shape[:2]
    y = y + xs.astype(jnp.float32) * d_skip.astype(jnp.float32)[:, None]
    g = (y.reshape(bsz, n, SSD_INNER) * jax.nn.silu(z.astype(jnp.float32)))
    g = g.reshape(bsz, n, SSD_GROUPS, SSD_INNER // SSD_GROUPS)
    g = g * lax.rsqrt(jnp.mean(g * g, axis=-1, keepdims=True) + EPS)
    return (g.reshape(bsz, n, SSD_INNER) * norm_w.astype(jnp.float32)).astype(z.dtype)


def mla_qkv(proj, q_a_norm_w, w_q_b, kv_a_norm_w, w_kv_b, rope):
    bsz, n = proj.shape[:2]
    cq = rmsnorm(proj[..., OFF_QA:OFF_KVA], q_a_norm_w)
    q = (cq @ w_q_b).reshape(bsz, n, MLA_HEADS, QK_DIM)
    ckv = rmsnorm(proj[..., OFF_KVA:OFF_KROPE], kv_a_norm_w)
    k_rope = proj[..., OFF_KROPE:OFF_POOL]
    kv = (ckv @ w_kv_b).reshape(bsz, n, MLA_HEADS, QK_NOPE + V_HEAD)
    k_nope, v = kv[..., :QK_NOPE], kv[..., QK_NOPE:]
    q_nope, q_rope = q[..., :QK_NOPE], q[..., QK_NOPE:]
    if rope is not None:
        cos, sin = rope
        q_rope = apply_axial_rope(q_rope, cos, sin)
        k_rope = apply_axial_rope(k_rope, cos, sin)
    q = jnp.concatenate([q_nope, q_rope], axis=-1)
    k = jnp.concatenate([k_nope, jnp.broadcast_to(k_rope[:, :, None, :], (bsz, n, MLA_HEADS, QK_ROPE))], axis=-1)
    return q, k, v


def attend(q, k, v):
    s = jnp.einsum('bqhd,bkhd->bhqk', q, k, preferred_element_type=jnp.float32) * (QK_DIM ** -0.5)
    p = jax.nn.softmax(s, axis=-1).astype(v.dtype)
    return jnp.einsum('bhqk,bkhd->bqhd', p, v)


def attend_blocked(q, k, v):
    bsz, n, h, dq = q.shape
    qb = jnp.moveaxis(q.reshape(bsz, n // Q_BLOCK, Q_BLOCK, h, dq), 1, 0)
    out = lax.map(lambda qi: attend(qi, k, v), qb)
    return jnp.moveaxis(out, 0, 1).reshape(bsz, n, h, v.shape[-1])


def multiscale_pool(u, pool_w, pool_scale):
    bsz, n, _ = u.shape
    uf = u.astype(jnp.float32)
    cs = jnp.pad(jnp.cumsum(uf, axis=1), ((0, 0), (1, 0), (0, 0)))
    t = jnp.arange(n)
    outs = []
    for gi, w in enumerate(POOL_WINDOWS):
        sl = slice(gi * POOL_GROUP_DIM, (gi + 1) * POOL_GROUP_DIM)
        lo = jnp.clip(t - w // 2, 0, n)
        hi = jnp.clip(t + w - w // 2, 0, n)
        csg = cs[..., sl]
        mean = (jnp.take(csg, hi, axis=1) - jnp.take(csg, lo, axis=1)) / (hi - lo).astype(jnp.float32)[:, None]
        outs.append(mean - uf[..., sl])
    d = jnp.stack(outs, axis=2)
    y = jnp.einsum('blgc,gcd->blgd', d, pool_w.astype(jnp.float32)).reshape(bsz, n, POOL_DIM)
    return (y * pool_scale.astype(jnp.float32)).astype(u.dtype)


def hybrid_layer(x, ctx, c, c_ctx, rope, mod_w, mod_b, norm1_w, norm2_w, w_in, conv_w, conv_b,
                 dt_bias, a_log, ssd_d, ssd_norm_w, q_a_norm_w, w_q_b, kv_a_norm_w, w_kv_b,
                 pool_w, pool_scale, w_out, w_mlp1, w_mlp2, update_ctx):
    bsz, n, _ = x.shape
    m = ctx.shape[1]
    sh1, sc1, g1, sh2, sc2, g2 = adaln(c, mod_w, mod_b)
    csh1, csc1, cg1, csh2, csc2, cg2 = adaln(c_ctx, mod_w, mod_b)
    px = modulate(rmsnorm(x, norm1_w), sh1, sc1) @ w_in
    pc = modulate(rmsnorm(ctx, norm1_w), csh1, csc1) @ w_in

    a = -jnp.exp(a_log.astype(jnp.float32))
    zc, xsc, bmc, cmc, dtc = ssd_inputs(pc, conv_w, conv_b, dt_bias)
    zx, xsx, bmx, cmx, dtx = ssd_inputs(px, conv_w, conv_b, dt_bias)
    h_zero = jnp.zeros((bsz, SSD_HEADS, SSD_HEAD_DIM, SSD_STATE), jnp.float32)
    yc, hf_ctx, hb_ctx = ssd_bidirectional(xsc, bmc, cmc, dtc, a, h_zero, h_zero)
    yx, _, _ = ssd_bidirectional(xsx, bmx, cmx, dtx, a, hf_ctx, hb_ctx)
    ssd_x = ssd_output(yx, xsx, zx, ssd_d, ssd_norm_w)

    qc, kc, vc = mla_qkv(pc, q_a_norm_w, w_q_b, kv_a_norm_w, w_kv_b, None)
    qx, kx, vx = mla_qkv(px, q_a_norm_w, w_q_b, kv_a_norm_w, w_kv_b, rope)
    attn_x = attend_blocked(qx, jnp.concatenate([kx, kc], axis=1), jnp.concatenate([vx, vc], axis=1))

    pool_x = multiscale_pool(px[..., OFF_POOL:], pool_w, pool_scale)

    mix_x = jnp.concatenate([ssd_x, attn_x.reshape(bsz, n, MLA_OUT), pool_x], axis=-1) @ w_out
    x = x + g1 * mix_x
    x = x + g2 * squared_relu_mlp(modulate(rmsnorm(x, norm2_w), sh2, sc2), w_mlp1, w_mlp2)

    if update_ctx:
        ssd_c = ssd_output(yc, xsc, zc, ssd_d, ssd_norm_w)
        attn_c = attend(qc, kc, vc).reshape(bsz, m, MLA_OUT)
        pool_c = multiscale_pool(pc[..., OFF_POOL:], pool_w, pool_scale)
        mix_c = jnp.concatenate([ssd_c, attn_c, pool_c], axis=-1) @ w_out
        ctx = ctx + cg1 * mix_c
        ctx = ctx + cg2 * squared_relu_mlp(modulate(rmsnorm(ctx, norm2_w), csh2, csc2), w_mlp1, w_mlp2)
    return x, ctx


def setup_inputs(seed: int = 0) -> dict:
    key = jax.random.key(seed)
    ks = jax.random.split(key, 32)
    f32 = jnp.float32

    def nrm(k, shape, scale):
        return jax.random.normal(k, shape, f32) * scale

    def gain(k, shape):
        return 1.0 + 0.02 * jax.random.normal(k, shape, f32)

    dt0 = jnp.exp(jax.random.uniform(ks[11], (DEPTH, 2, SSD_HEADS), f32, math.log(1e-3), math.log(1e-1)))
    return {
        "x": nrm(ks[0], (BATCH, SEQ, D_MODEL), 1.0),
        "c": nrm(ks[1], (BATCH, D_MODEL), 1.0),
        "ctx": nrm(ks[2], (BATCH, CTX_LEN, D_MODEL), 1.0),
        "c_ctx": nrm(ks[3], (D_MODEL,), 1.0),
        "mod_w": nrm(ks[4], (DEPTH, D_MODEL, 6 * D_MODEL), 0.5 * D_MODEL ** -0.5),
        "mod_b": nrm(ks[5], (DEPTH, 6 * D_MODEL), 0.01),
        "norm1_w": gain(ks[6], (DEPTH, D_MODEL)),
        "norm2_w": gain(ks[7], (DEPTH, D_MODEL)),
        "w_in": nrm(ks[8], (DEPTH, D_MODEL, IN_COLS), D_MODEL ** -0.5),
        "conv_w": nrm(ks[9], (DEPTH, SSD_CONV, XBC_DIM), SSD_CONV ** -0.5),
        "conv_b": nrm(ks[10], (DEPTH, XBC_DIM), 0.01),
        "dt_bias": dt0 + jnp.log(-jnp.expm1(-dt0)),
        "a_log": jnp.log(jax.random.uniform(ks[12], (DEPTH, 2, SSD_HEADS), f32, 1.0, 16.0)),
        "ssd_d": 1.0 + 0.1 * jax.random.normal(ks[13], (DEPTH, SSD_HEADS), f32),
        "ssd_norm_w": gain(ks[14], (DEPTH, SSD_INNER)),
        "q_a_norm_w": gain(ks[15], (DEPTH, Q_LORA)),
        "w_q_b": nrm(ks[16], (DEPTH, Q_LORA, MLA_HEADS * QK_DIM), Q_LORA ** -0.5),
        "kv_a_norm_w": gain(ks[17], (DEPTH, KV_LORA)),
        "w_kv_b": nrm(ks[18], (DEPTH, KV_LORA, MLA_HEADS * (QK_NOPE + V_HEAD)), KV_LORA ** -0.5),
        "pool_w": nrm(ks[19], (DEPTH, POOL_GROUPS, POOL_GROUP_DIM, POOL_GROUP_DIM), POOL_GROUP_DIM ** -0.5),
        "pool_scale": gain(ks[20], (DEPTH, POOL_DIM)),
        "w_out": nrm(ks[21], (DEPTH, MIX_DIM, D_MODEL), MIX_DIM ** -0.5),
        "w_mlp1": nrm(ks[22], (DEPTH, D_MODEL, D_FF), D_MODEL ** -0.5),
        "w_mlp2": nrm(ks[23], (DEPTH, D_FF, D_MODEL), D_FF ** -0.5),
        "final_norm_w": gain(ks[24], (D_MODEL,)),
    }


def reference(x, c, ctx, c_ctx, mod_w, mod_b, norm1_w, norm2_w, w_in, conv_w, conv_b, dt_bias, a_log,
              ssd_d, ssd_norm_w, q_a_norm_w, w_q_b, kv_a_norm_w, w_kv_b, pool_w, pool_scale, w_out,
              w_mlp1, w_mlp2, final_norm_w):
    rope = axial_rope_tables(x.shape[1])
    for i in range(DEPTH):
        x, ctx = hybrid_layer(x, ctx, c, c_ctx, rope, mod_w[i], mod_b[i], norm1_w[i], norm2_w[i], w_in[i],
                              conv_w[i], conv_b[i], dt_bias[i], a_log[i], ssd_d[i], ssd_norm_w[i],
                              q_a_norm_w[i], w_q_b[i], kv_a_norm_w[i], w_kv_b[i], pool_w[i], pool_scale[i],
                              w_out[i], w_mlp1[i], w_mlp2[i], update_ctx=(i < DEPTH - 1))
    return rmsnorm(x, final_norm_w)
```

```python
import functools
import math

import jax
import jax.numpy as jnp
import numpy as np
from jax import lax
from jax.experimental import pallas as pl
from jax.experimental.pallas import tpu as pltpu

F32 = jnp.float32
BF16 = jnp.bfloat16

D_MODEL = 1024
GRID_W = 64
EPS = 1e-6

SSD_HEADS = 6
SSD_HEAD_DIM = 64
SSD_GROUPS = 2
SSD_STATE = 128
SSD_CONV = 4
SSD_CHUNK = 128
HEADS_PER_GROUP = SSD_HEADS // SSD_GROUPS
GROUP_DIM = HEADS_PER_GROUP * SSD_HEAD_DIM
GROUP_PAD = 256
SSD_INNER = SSD_HEADS * SSD_HEAD_DIM
SSD_PAD = SSD_GROUPS * GROUP_PAD
XBC_DIM = SSD_INNER + 2 * SSD_GROUPS * SSD_STATE
XBC_PAD = SSD_PAD + 2 * SSD_GROUPS * SSD_STATE

MLA_HEADS = 6
Q_LORA = 256
KV_LORA = 256
QK_NOPE = 64
QK_ROPE = 32
V_HEAD = 64
QK_DIM = QK_NOPE + QK_ROPE
MLA_OUT = MLA_HEADS * V_HEAD
ROPE_THETA = 10000.0
ROPE_PAIRS = QK_ROPE // 4
HEAD_SLAB = 128

POOL_WINDOWS = (2, 4, 8, 16)
POOL_GROUP_DIM = 64
POOL_DIM = len(POOL_WINDOWS) * POOL_GROUP_DIM
POOL_HALO = 8
CONV_HALO = 8

D_FF = 4 * D_MODEL
FF_CHUNK = 512

OFF_Z = 0
OFF_XBC = OFF_Z + SSD_INNER
OFF_DT = OFF_XBC + XBC_DIM
OFF_QA = OFF_DT + 2 * SSD_HEADS
OFF_KVA = OFF_QA + Q_LORA
OFF_KROPE = OFF_KVA + KV_LORA
OFF_POOL = OFF_KROPE + QK_ROPE
IN_COLS = OFF_POOL + POOL_DIM

P_Z = 0
P_XBC = P_Z + SSD_PAD
P_QA = P_XBC + XBC_PAD
P_KVA = P_QA + Q_LORA
P_POOL = P_KVA + KV_LORA
P_DT = P_POOL + POOL_DIM
P_KR = P_DT + 128
P_COLS = P_KR + HEAD_SLAB

TM = 256
ATT_TQ = 256
ATT_TK = 512
NEG_BIG = -1e30
VMEM_LIMIT = 56 * 1024 * 1024
LOG2E = math.log2(math.e)


def _pad_group_index(base):
    idx = np.full((SSD_PAD,), -1, np.int64)
    for g in range(SSD_GROUPS):
        idx[g * GROUP_PAD:g * GROUP_PAD + GROUP_DIM] = base + g * GROUP_DIM + np.arange(GROUP_DIM)
    return idx


def _rope_perm():
    src = np.zeros((QK_ROPE,), np.int64)
    for half in range(2):
        for axis in range(2):
            for p in range(ROPE_PAIRS):
                src[half * 16 + axis * 8 + p] = axis * 16 + half * 8 + p
    return src


def _win_index():
    idx = np.full((P_COLS,), -1, np.int64)
    idx[P_Z:P_Z + SSD_PAD] = _pad_group_index(OFF_Z)
    idx[P_XBC:P_XBC + SSD_PAD] = _pad_group_index(OFF_XBC)
    nbc = 2 * SSD_GROUPS * SSD_STATE
    idx[P_XBC + SSD_PAD:P_XBC + SSD_PAD + nbc] = OFF_XBC + SSD_INNER + np.arange(nbc)
    idx[P_QA:P_QA + Q_LORA] = OFF_QA + np.arange(Q_LORA)
    idx[P_KVA:P_KVA + KV_LORA] = OFF_KVA + np.arange(KV_LORA)
    idx[P_POOL:P_POOL + POOL_DIM] = OFF_POOL + np.arange(POOL_DIM)
    idx[P_DT:P_DT + 2 * SSD_HEADS] = OFF_DT + np.arange(2 * SSD_HEADS)
    idx[P_KR + QK_NOPE:P_KR + QK_NOPE + QK_ROPE] = OFF_KROPE + _rope_perm()
    return idx


def _wq_index():
    idx = np.full((MLA_HEADS * HEAD_SLAB,), -1, np.int64)
    perm = _rope_perm()
    for h in range(MLA_HEADS):
        idx[h * HEAD_SLAB:h * HEAD_SLAB + QK_NOPE] = h * QK_DIM + np.arange(QK_NOPE)
        idx[h * HEAD_SLAB + QK_NOPE:h * HEAD_SLAB + QK_DIM] = h * QK_DIM + QK_NOPE + perm
    return idx


def _wkv_index():
    nk = MLA_HEADS * HEAD_SLAB
    idx = np.full((2 * nk,), -1, np.int64)
    per_head = QK_NOPE + V_HEAD
    for h in range(MLA_HEADS):
        idx[h * HEAD_SLAB:h * HEAD_SLAB + QK_NOPE] = h * per_head + np.arange(QK_NOPE)
        lo = nk + h * HEAD_SLAB + (h % 2) * V_HEAD
        idx[lo:lo + V_HEAD] = h * per_head + QK_NOPE + np.arange(V_HEAD)
    return idx


def _remap_cols(w, idx):
    take = jnp.take(w, jnp.asarray(np.maximum(idx, 0)), axis=-1)
    return jnp.where(jnp.asarray(idx >= 0), take, jnp.zeros((), w.dtype))


def _rope_tables(seq, ctx):
    t = np.arange(seq)
    row = jnp.asarray((t // GRID_W).astype(np.float32))
    col = jnp.asarray((t % GRID_W).astype(np.float32))
    inv_freq = ROPE_THETA ** (-jnp.arange(ROPE_PAIRS, dtype=F32) / ROPE_PAIRS)
    ang = jnp.concatenate([row[:, None] * inv_freq, col[:, None] * inv_freq], axis=1)
    cos16, sin16 = jnp.cos(ang), jnp.sin(ang)
    ones = jnp.ones((seq, QK_NOPE), F32)
    zeros = jnp.zeros((seq, QK_NOPE), F32)
    z16 = jnp.zeros((seq, 16), F32)
    tail1 = jnp.ones((seq, HEAD_SLAB - QK_DIM), F32)
    tail0 = jnp.zeros((seq, HEAD_SLAB - QK_DIM), F32)
    c_tab = jnp.concatenate([ones, cos16, cos16, tail1], axis=1)
    s1_tab = jnp.concatenate([zeros, -sin16, z16, tail0], axis=1)
    s2_tab = jnp.concatenate([zeros, z16, sin16, tail0], axis=1)
    c_tab = jnp.concatenate([c_tab, jnp.ones((ctx, HEAD_SLAB), F32)], axis=0)
    s1_tab = jnp.concatenate([s1_tab, jnp.zeros((ctx, HEAD_SLAB), F32)], axis=0)
    s2_tab = jnp.concatenate([s2_tab, jnp.zeros((ctx, HEAD_SLAB), F32)], axis=0)
    k_tabs = jnp.stack([c_tab, s1_tab, s2_tab])
    qscale = (QK_DIM ** -0.5) * LOG2E
    return jnp.concatenate([k_tabs, k_tabs * qscale], axis=0)


def _rms(u, w_row):
    ms = jnp.mean(u * u, axis=-1, keepdims=True)
    return u * lax.rsqrt(ms + EPS) * w_row


def _silu(u):
    return u * (1.0 / (1.0 + jnp.exp(-u)))


def _softplus(u):
    return jnp.maximum(u, 0.0) + jnp.log1p(jnp.exp(-jnp.abs(u)))


def _mod_kernel(cond_ref, w_ref, b_ref, o_ref):
    cnd = cond_ref[...]
    o_ref[0] = jnp.dot(_silu(cnd), w_ref[0], preferred_element_type=F32,
                       precision=lax.Precision.HIGHEST) + b_ref[0]


def _modulation(cond, mod_w, mod_b):
    depth = mod_w.shape[0]
    r = cond.shape[0]
    out = pl.pallas_call(
        _mod_kernel,
        grid=(depth, 6),
        in_specs=[pl.BlockSpec((r, D_MODEL), lambda d, j: (0, 0)),
                  pl.BlockSpec((1, D_MODEL, D_MODEL), lambda d, j: (d, 0, j)),
                  pl.BlockSpec((1, 1, D_MODEL), lambda d, j: (d, 0, j))],
        out_specs=pl.BlockSpec((1, r, D_MODEL), lambda d, j: (d, 0, j)),
        out_shape=jax.ShapeDtypeStruct((depth, r, 6 * D_MODEL), F32),
        name="adaln_modulation",
    )(cond, mod_w, mod_b.reshape(depth, 1, 6 * D_MODEL))
    return out.reshape(depth, r, 6, D_MODEL)


def _rope(t, c, s1, s2):
    return t * c + pltpu.roll(t, HEAD_SLAB - 16, 1) * s1 + pltpu.roll(t, 16, 1) * s2


def _inproj_kernel(x_ref, mod_ref, n1_ref, win_ref, qan_ref, kvan_ref, wq_ref, wkv_ref, rope_ref,
                   z_ref, xbc_ref, dt_ref, pool_ref, q_ref, k_ref, v_ref):
    h = _rms(x_ref[0], n1_ref[...])
    h = h * (1.0 + mod_ref[0, 1:2, :]) + mod_ref[0, 0:1, :]
    hb = h.astype(BF16)

    def proj(lo, width):
        return jnp.dot(hb, win_ref[:, lo:lo + width], preferred_element_type=F32)

    z_ref[0] = proj(P_Z, SSD_PAD)
    xbc_ref[0] = proj(P_XBC, XBC_PAD)
    pool_ref[0] = proj(P_POOL, POOL_DIM)
    dt_ref[0] = proj(P_DT, 128)

    cq = _rms(proj(P_QA, Q_LORA), qan_ref[...]).astype(BF16)
    ckv = _rms(proj(P_KVA, KV_LORA), kvan_ref[...]).astype(BF16)
    q = jnp.dot(cq, wq_ref[...], preferred_element_type=F32)
    kv = jnp.dot(ckv, wkv_ref[...], preferred_element_type=F32)
    k_rope = _rope(proj(P_KR, HEAD_SLAB), rope_ref[0], rope_ref[1], rope_ref[2])
    for hd in range(MLA_HEADS):
        sl = slice(hd * HEAD_SLAB, (hd + 1) * HEAD_SLAB)
        q_ref[0, :, sl] = _rope(q[:, sl], rope_ref[3], rope_ref[4], rope_ref[5]).astype(BF16)
        k_ref[0, :, sl] = (kv[:, sl] + k_rope).astype(BF16)
    v_ref[0] = kv[:, MLA_HEADS * HEAD_SLAB:].astype(BF16)


def _inproj(xc, mod, n1, win, qan, kvan, wq, wkv, rope, n_lat_tiles):
    bsz, t, _ = xc.shape
    nt = t // TM
    ctx_row = mod.shape[0] - 1

    def tok(width):
        return pl.BlockSpec((1, TM, width), lambda b, i: (b, i, 0))

    def const(shape):
        return pl.BlockSpec(shape, lambda b, i: (0,) * len(shape))

    mod_spec = pl.BlockSpec((1, 6, D_MODEL), lambda b, i: (jnp.where(i >= n_lat_tiles, ctx_row, b), 0, 0))
    nh = MLA_HEADS * HEAD_SLAB
    outs = [(SSD_PAD, F32), (XBC_PAD, F32), (128, F32), (POOL_DIM, F32), (nh, BF16), (nh, BF16), (nh, BF16)]
    return pl.pallas_call(
        _inproj_kernel,
        grid=(bsz, nt),
        in_specs=[tok(D_MODEL), mod_spec, const((1, D_MODEL)), const((D_MODEL, P_COLS)),
                  const((1, Q_LORA)), const((1, KV_LORA)), const((Q_LORA, nh)), const((KV_LORA, 2 * nh)),
                  pl.BlockSpec((6, TM, HEAD_SLAB), lambda b, i: (0, i, 0))],
        out_specs=[tok(w) for w, _ in outs],
        out_shape=[jax.ShapeDtypeStruct((bsz, t, w), dt) for w, dt in outs],
        compiler_params=pltpu.CompilerParams(dimension_semantics=("parallel", "parallel"),
                                             vmem_limit_bytes=VMEM_LIMIT),
        name="inproj_mla",
    )(xc, mod, n1, win, qan, kvan, wq, wkv, rope)


def _expand_heads(cols, first_lane):
    n = cols.shape[0]
    lane = lax.broadcasted_iota(jnp.int32, (n, 128), 1)
    parts = []
    for g in range(SSD_GROUPS):
        b = [jnp.broadcast_to(cols[:, first_lane + HEADS_PER_GROUP * g + i:first_lane + HEADS_PER_GROUP * g + i + 1],
                              (n, 128)) for i in range(HEADS_PER_GROUP)]
        parts.append(jnp.where(lane < SSD_HEAD_DIM, b[0], b[1]))
        parts.append(b[2])
    return jnp.concatenate(parts, axis=1)


def _ssd_kernel(*refs, rev, n_lat_chunks, n_chunks):
    if rev:
        (xbc_ref, xprev_ref, xnext_ref, dt_ref, convw_ref, convb_ref, dtb_ref, alog_ref,
         out_ref, ext_scr, h_scr) = refs
    else:
        (xbc_ref, xprev_ref, xnext_ref, dt_ref, z_ref, yb_ref, convw_ref, convb_ref, dtb_ref, alog_ref,
         dskip_ref, nw_ref, out_ref, ext_scr, h_scr) = refs
    L = SSD_CHUNK
    j = pl.program_id(1)
    c = _ssd_chunk(j, rev, n_lat_chunks, n_chunks)

    @pl.when(j == 0)
    def _():
        h_scr[...] = jnp.zeros_like(h_scr)

    seg_first = jnp.logical_or(c == 0, c == n_lat_chunks)
    seg_last = jnp.logical_or(c == n_lat_chunks - 1, c == n_chunks - 1)
    ext_scr[0:CONV_HALO, :] = xprev_ref[0] * jnp.where(seg_first, 0.0, 1.0)
    ext_scr[CONV_HALO:CONV_HALO + L, :] = xbc_ref[0]
    ext_scr[CONV_HALO + L:CONV_HALO + L + CONV_HALO, :] = xnext_ref[0] * jnp.where(seg_last, 0.0, 1.0)
    u = convb_ref[...]
    for tap in range(SSD_CONV):
        lo = CONV_HALO - 1 + tap
        u = u + ext_scr[lo:lo + L, :] * convw_ref[tap:tap + 1, :]
    u = _silu(u)
    xs = u[:, :SSD_PAD]
    bmat = u[:, SSD_PAD:SSD_PAD + SSD_GROUPS * SSD_STATE].astype(BF16)
    cmat = u[:, SSD_PAD + SSD_GROUPS * SSD_STATE:].astype(BF16)

    first_lane = SSD_HEADS if rev else 0
    dtv = _softplus(dt_ref[0] + dtb_ref[...])
    adt = dtv * (-jnp.exp(alog_ref[...]))
    ri = lax.broadcasted_iota(jnp.int32, (L, L), 0)
    ci = lax.broadcasted_iota(jnp.int32, (L, L), 1)
    causal = (ci >= ri) if rev else (ci <= ri)
    cs_col = jnp.dot(causal.astype(F32), adt, preferred_element_type=F32,
                     precision=lax.Precision.HIGHEST)
    cs_row = cs_col.T

    dt_e = _expand_heads(dtv, first_lane)
    cs_e = _expand_heads(cs_col, first_lane)
    end_row = 0 if rev else L - 1
    cs_end = cs_e[end_row:end_row + 1, :]
    xdt = xs * dt_e
    xdt_b = xdt.astype(BF16)
    xdd_b = (xdt * jnp.exp(cs_end - cs_e)).astype(BF16)
    in_decay = jnp.exp(cs_e)
    state_decay = jnp.exp(cs_end)

    lane_g = lax.broadcasted_iota(jnp.int32, (1, GROUP_PAD), 1)
    ys = []
    for g in range(SSD_GROUPS):
        gs = slice(g * GROUP_PAD, (g + 1) * GROUP_PAD)
        ns = slice(g * SSD_STATE, (g + 1) * SSD_STATE)
        cm_g, bm_g = cmat[:, ns], bmat[:, ns]
        cb = lax.dot_general(cm_g, bm_g, (((1,), (1,)), ((), ())), preferred_element_type=F32)
        x_g = xdt_b[:, gs]
        y_g = jnp.zeros((L, GROUP_PAD), F32)
        for i in range(HEADS_PER_GROUP):
            hl = first_lane + HEADS_PER_GROUP * g + i
            diff = cs_col[:, hl:hl + 1] - cs_row[hl:hl + 1, :]
            decay = jnp.exp(jnp.where(causal, diff, NEG_BIG))
            head_lanes = jnp.logical_and(lane_g >= i * SSD_HEAD_DIM, lane_g < (i + 1) * SSD_HEAD_DIM)
            x_h = jnp.where(head_lanes, x_g, jnp.zeros((), BF16))
            y_g = y_g + jnp.dot((cb * decay).astype(BF16), x_h, preferred_element_type=F32)
        h_t = h_scr[g]
        y_in = jnp.dot(cm_g, h_t.astype(BF16), preferred_element_type=F32)
        y_g = y_g + y_in * in_decay[:, gs]
        s_new = lax.dot_general(bm_g, xdd_b[:, gs], (((0,), (0,)), ((), ())), preferred_element_type=F32)
        h_scr[g] = h_t * state_decay[:, gs] + s_new
        ys.append(y_g)
    y = jnp.concatenate(ys, axis=1)

    if rev:
        out_ref[0] = y
    else:
        y = y + yb_ref[0] + xs * dskip_ref[...]
        gated = y * _silu(z_ref[0])
        outs = []
        for g in range(SSD_GROUPS):
            sl = gated[:, g * GROUP_PAD:(g + 1) * GROUP_PAD]
            ms = jnp.sum(sl * sl, axis=-1, keepdims=True) * (1.0 / GROUP_DIM)
            outs.append(sl * lax.rsqrt(ms + EPS))
        out_ref[0] = (jnp.concatenate(outs, axis=1) * nw_ref[...]).astype(BF16)


def _ssd_chunk(j, rev, n_lat_chunks, n_chunks):
    if rev:
        return n_chunks - 1 - j
    n_ctx = n_chunks - n_lat_chunks
    return jnp.where(j < n_ctx, n_lat_chunks + j, j - n_ctx)


def _ssd_scan(xbc, dt, z, yb, convw, convb, dtb, alog, dskip, nw, n_lat_chunks, rev):
    bsz, t, _ = xbc.shape
    n_chunks = t // SSD_CHUNK
    halo_blocks = t // CONV_HALO
    per = SSD_CHUNK // CONV_HALO
    chunk = functools.partial(_ssd_chunk, rev=rev, n_lat_chunks=n_lat_chunks, n_chunks=n_chunks)

    def tok(width):
        return pl.BlockSpec((1, SSD_CHUNK, width), lambda b, j: (b, chunk(j), 0))

    def const(shape):
        return pl.BlockSpec(shape, lambda b, j: (0,) * len(shape))

    prev_spec = pl.BlockSpec((1, CONV_HALO, XBC_PAD), lambda b, j: (b, jnp.maximum(chunk(j) * per - 1, 0), 0))
    next_spec = pl.BlockSpec((1, CONV_HALO, XBC_PAD),
                             lambda b, j: (b, jnp.minimum((chunk(j) + 1) * per, halo_blocks - 1), 0))
    params = [convw, convb, dtb, alog]
    param_specs = [const(p.shape) for p in params]
    if rev:
        args = [xbc, xbc, xbc, dt] + params
        in_specs = [tok(XBC_PAD), prev_spec, next_spec, tok(128)] + param_specs
        out_dtype = F32
    else:
        args = [xbc, xbc, xbc, dt, z, yb] + params + [dskip, nw]
        in_specs = ([tok(XBC_PAD), prev_spec, next_spec, tok(128), tok(SSD_PAD), tok(SSD_PAD)] + param_specs
                    + [const(dskip.shape), const(nw.shape)])
        out_dtype = BF16
    return pl.pallas_call(
        functools.partial(_ssd_kernel, rev=rev, n_lat_chunks=n_lat_chunks, n_chunks=n_chunks),
        grid=(bsz, n_chunks),
        in_specs=in_specs,
        out_specs=tok(SSD_PAD),
        out_shape=jax.ShapeDtypeStruct((bsz, t, SSD_PAD), out_dtype),
        scratch_shapes=[pltpu.VMEM((SSD_CHUNK + 2 * CONV_HALO, XBC_PAD), F32),
                        pltpu.VMEM((SSD_GROUPS, SSD_STATE, GROUP_PAD), F32)],
        compiler_params=pltpu.CompilerParams(dimension_semantics=("parallel", "arbitrary"),
                                             vmem_limit_bytes=VMEM_LIMIT),
        name="ssd_bwd" if rev else "ssd_fwd",
    )(*args)


def _attn_kernel(q_ref, k_ref, v_ref, o_ref, *, seq, ctx):
    tq = q_ref.shape[1]
    lat_chunks = [(lo, min(ATT_TK, seq - lo)) for lo in range(0, seq, ATT_TK)]
    ctx_chunks = [(seq + lo, min(ATT_TK, ctx - lo)) for lo in range(0, ctx, ATT_TK)]
    lane = lax.broadcasted_iota(jnp.int32, (1, HEAD_SLAB), 1)
    low_half = lane < V_HEAD

    def run(chunks):
        m = [jnp.full((tq, 1), -jnp.inf, F32) for _ in range(2)]
        l = [jnp.zeros((tq, 1), F32) for _ in range(2)]
        acc = jnp.zeros((tq, HEAD_SLAB), F32)
        for lo, size in chunks:
            alpha, pv = [], []
            for hd in range(2):
                sl = slice(hd * HEAD_SLAB, (hd + 1) * HEAD_SLAB)
                s = lax.dot_general(q_ref[0, :, sl], k_ref[0, lo:lo + size, sl], (((1,), (1,)), ((), ())),
                                    preferred_element_type=F32)
                m_new = jnp.maximum(m[hd], jnp.max(s, axis=-1, keepdims=True))
                a = jnp.exp2(m[hd] - m_new)
                p = jnp.exp2(s - m_new)
                l[hd] = a * l[hd] + jnp.sum(p, axis=-1, keepdims=True)
                m[hd] = m_new
                alpha.append(a)
                pv.append(jnp.dot(p.astype(BF16), v_ref[0, lo:lo + size, sl], preferred_element_type=F32))
            acc = acc * jnp.where(low_half, alpha[0], alpha[1]) + pv[0] + pv[1]
        o_ref[0] = (acc * jnp.where(low_half, 1.0 / l[0], 1.0 / l[1])).astype(BF16)

    is_ctx = pl.program_id(2) >= seq // tq

    @pl.when(jnp.logical_not(is_ctx))
    def _():
        run(lat_chunks + ctx_chunks)

    @pl.when(is_ctx)
    def _():
        run(ctx_chunks)


def _attention(q, k, v, seq, ctx):
    bsz, t, _ = q.shape
    pairs = MLA_HEADS // 2
    return pl.pallas_call(
        functools.partial(_attn_kernel, seq=seq, ctx=ctx),
        grid=(bsz, pairs, t // ATT_TQ),
        in_specs=[pl.BlockSpec((1, ATT_TQ, 2 * HEAD_SLAB), lambda b, p, i: (b, i, p)),
                  pl.BlockSpec((1, t, 2 * HEAD_SLAB), lambda b, p, i: (b, 0, p)),
                  pl.BlockSpec((1, t, 2 * HEAD_SLAB), lambda b, p, i: (b, 0, p))],
        out_specs=pl.BlockSpec((1, ATT_TQ, HEAD_SLAB), lambda b, p, i: (b, i, p)),
        out_shape=jax.ShapeDtypeStruct((bsz, t, MLA_OUT), BF16),
        compiler_params=pltpu.CompilerParams(dimension_semantics=("parallel", "parallel", "arbitrary"),
                                             vmem_limit_bytes=VMEM_LIMIT),
        name="mla_attention",
    )(q, k, v)


def _pool_mix(pool_ref, pprev_ref, pnext_ref, ext_scr, pw_ref, ps_ref, tile, n_lat_tiles, n_tiles, seq, ctx):
    seg_first = jnp.logical_or(tile == 0, tile == n_lat_tiles)
    seg_last = jnp.logical_or(tile == n_lat_tiles - 1, tile == n_tiles - 1)
    h0 = POOL_HALO
    ext_scr[0:h0, :] = pprev_ref[0] * jnp.where(seg_first, 0.0, 1.0)
    ext_scr[h0:h0 + TM, :] = pool_ref[0]
    ext_scr[h0 + TM:h0 + TM + h0, :] = pnext_ref[0] * jnp.where(seg_last, 0.0, 1.0)

    is_ctx = tile >= n_lat_tiles
    seg_len = jnp.where(is_ctx, ctx, seq)
    pos = lax.broadcasted_iota(jnp.int32, (TM, 1), 0) + jnp.where(is_ctx, tile - n_lat_tiles, tile) * TM
    lane = lax.broadcasted_iota(jnp.int32, (1, 128), 1)
    low_half = lane < POOL_GROUP_DIM

    def inv_count(w):
        cnt = jnp.minimum(pos + (w - w // 2), seg_len) - jnp.maximum(pos - w // 2, 0)
        return 1.0 / cnt.astype(F32)

    def taps(slab, offsets):
        tot = None
        for k in offsets:
            piece = ext_scr[h0 + k:h0 + k + TM, slab * 128:(slab + 1) * 128]
            tot = piece if tot is None else tot + piece
        return tot

    outs = []
    for slab in range(2):
        w_small, w_big = POOL_WINDOWS[2 * slab], POOL_WINDOWS[2 * slab + 1]
        small = taps(slab, range(-(w_small // 2), w_small - w_small // 2))
        extra = [k for k in range(-(w_big // 2), w_big - w_big // 2)
                 if not -(w_small // 2) <= k < w_small - w_small // 2]
        big = small + taps(slab, extra)
        mean = jnp.where(low_half, small * inv_count(w_small), big * inv_count(w_big))
        outs.append(mean - ext_scr[h0:h0 + TM, slab * 128:(slab + 1) * 128])
    d = jnp.concatenate(outs, axis=1).astype(BF16)
    return jnp.dot(d, pw_ref[...], preferred_element_type=F32) * ps_ref[...]


def _mix_mlp_kernel(*refs, final, n_lat_tiles, n_tiles, seq, ctx):
    if final:
        (x_ref, ssd_ref, attn_ref, pool_ref, pprev_ref, pnext_ref, mod_ref, wos_ref, woa_ref, wop_ref,
         pw_ref, ps_ref, n2_ref, w1_ref, w2_ref, fn_ref, o_ref, ext_scr) = refs
    else:
        (x_ref, ssd_ref, attn_ref, pool_ref, pprev_ref, pnext_ref, mod_ref, wos_ref, woa_ref, wop_ref,
         pw_ref, ps_ref, n2_ref, w1_ref, w2_ref, o_ref, ext_scr) = refs
    tile = pl.program_id(1)
    pool_y = _pool_mix(pool_ref, pprev_ref, pnext_ref, ext_scr, pw_ref, ps_ref, tile,
                       n_lat_tiles, n_tiles, seq, ctx)
    mix = (jnp.dot(ssd_ref[0], wos_ref[...], preferred_element_type=F32)
           + jnp.dot(attn_ref[0], woa_ref[...], preferred_element_type=F32)
           + jnp.dot(pool_y.astype(BF16), wop_ref[...], preferred_element_type=F32))
    x1 = x_ref[0] + mod_ref[0, 2:3, :] * mix
    h = _rms(x1, n2_ref[...])
    hb = (h * (1.0 + mod_ref[0, 4:5, :]) + mod_ref[0, 3:4, :]).astype(BF16)
    acc = jnp.zeros((TM, D_MODEL), F32)
    for lo in range(0, D_FF, FF_CHUNK):
        a = jnp.maximum(jnp.dot(hb, w1_ref[:, lo:lo + FF_CHUNK], preferred_element_type=F32), 0.0)
        acc = acc + jnp.dot((a * a).astype(BF16), w2_ref[lo:lo + FF_CHUNK, :], preferred_element_type=F32)
    x2 = x1 + mod_ref[0, 5:6, :] * acc
    if final:
        x2 = _rms(x2, fn_ref[...])
    o_ref[0] = x2


def _mix_mlp(xc, ssd, attn, pool, mod, wos, woa, wop, pw, ps, n2, w1, w2, fn, n_lat_tiles, seq, ctx, final):
    bsz, t, _ = xc.shape
    n_tiles = t // TM
    out_tiles = n_lat_tiles if final else n_tiles
    halo_blocks = t // POOL_HALO
    per = TM // POOL_HALO
    ctx_row = mod.shape[0] - 1

    def tok(width):
        return pl.BlockSpec((1, TM, width), lambda b, i: (b, i, 0))

    def const(shape):
        return pl.BlockSpec(shape, lambda b, i: (0,) * len(shape), pipeline_mode=pl.Buffered(1))

    prev_spec = pl.BlockSpec((1, POOL_HALO, POOL_DIM), lambda b, i: (b, jnp.maximum(i * per - 1, 0), 0))
    next_spec = pl.BlockSpec((1, POOL_HALO, POOL_DIM),
                             lambda b, i: (b, jnp.minimum((i + 1) * per, halo_blocks - 1), 0))
    mod_spec = pl.BlockSpec((1, 6, D_MODEL), lambda b, i: (jnp.where(i >= n_lat_tiles, ctx_row, b), 0, 0))
    weights = [wos, woa, wop, pw, ps, n2, w1, w2] + ([fn] if final else [])
    return pl.pallas_call(
        functools.partial(_mix_mlp_kernel, final=final, n_lat_tiles=n_lat_tiles, n_tiles=n_tiles,
                          seq=seq, ctx=ctx),
        grid=(bsz, out_tiles),
        in_specs=[tok(D_MODEL), tok(SSD_PAD), tok(MLA_OUT), tok(POOL_DIM), prev_spec, next_spec, mod_spec]
                 + [const(w.shape) for w in weights],
        out_specs=tok(D_MODEL),
        out_shape=jax.ShapeDtypeStruct((bsz, out_tiles * TM, D_MODEL), F32),
        scratch_shapes=[pltpu.VMEM((TM + 2 * POOL_HALO, POOL_DIM), F32)],
        compiler_params=pltpu.CompilerParams(dimension_semantics=("parallel", "parallel"),
                                             vmem_limit_bytes=VMEM_LIMIT),
        name="mix_mlp_final" if final else "mix_mlp",
    )(xc, ssd, attn, pool, pool, pool, mod, *weights)


def kernel(x, c, ctx, c_ctx, mod_w, mod_b, norm1_w, norm2_w, w_in, conv_w, conv_b, dt_bias, a_log, ssd_d,
           ssd_norm_w, q_a_norm_w, w_q_b, kv_a_norm_w, w_kv_b, pool_w, pool_scale, w_out, w_mlp1, w_mlp2,
           final_norm_w):
    bsz, seq, _ = x.shape
    n_ctx = ctx.shape[1]
    depth = mod_w.shape[0]
    assert seq % TM == 0 and n_ctx % TM == 0 and seq % GRID_W == 0
    assert seq % SSD_CHUNK == 0 and n_ctx % SSD_CHUNK == 0
    n_lat_tiles = seq // TM
    n_lat_chunks = seq // SSD_CHUNK

    xc = jnp.concatenate([x, ctx], axis=1)
    cond_rows = -(-(bsz + 1) // 8) * 8
    cond = jnp.zeros((cond_rows, D_MODEL), F32).at[:bsz].set(c).at[bsz].set(c_ctx)
    mod_all = _modulation(cond, mod_w, mod_b)[:, :bsz + 1]
    rope = _rope_tables(seq, n_ctx)

    win_idx, wq_idx, wkv_idx = _win_index(), _wq_index(), _wkv_index()
    pad_idx = _pad_group_index(0)
    xbc_idx = np.concatenate([pad_idx, SSD_INNER + np.arange(XBC_DIM - SSD_INNER)])
    row = lambda v: v.reshape(1, -1)

    for i in range(depth):
        final = i == depth - 1
        win = _remap_cols(w_in[i], win_idx).astype(BF16)
        wq = _remap_cols(w_q_b[i], wq_idx).astype(BF16)
        wkv = _remap_cols(w_kv_b[i], wkv_idx).astype(BF16)
        convw = _remap_cols(conv_w[i], xbc_idx)
        convb = row(_remap_cols(conv_b[i], xbc_idx))
        dtb = row(jnp.pad(dt_bias[i].reshape(-1), (0, 128 - 2 * SSD_HEADS)))
        alog = row(jnp.pad(a_log[i].reshape(-1), (0, 128 - 2 * SSD_HEADS)))
        dskip = row(_remap_cols(jnp.repeat(ssd_d[i], SSD_HEAD_DIM), pad_idx))
        nw = row(_remap_cols(ssd_norm_w[i], pad_idx))
        wo = w_out[i].astype(BF16)
        wos = _remap_cols(wo[:SSD_INNER].T, pad_idx).T
        woa = wo[SSD_INNER:SSD_INNER + MLA_OUT]
        wop = wo[SSD_INNER + MLA_OUT:]
        pw = jax.scipy.linalg.block_diag(*[pool_w[i, g] for g in range(len(POOL_WINDOWS))]).astype(BF16)

        z, xbc, dt, pool, q, k, v = _inproj(xc, mod_all[i], row(norm1_w[i]), win, row(q_a_norm_w[i]),
                                             row(kv_a_norm_w[i]), wq, wkv, rope, n_lat_tiles)
        yb = _ssd_scan(xbc, dt, None, None, convw, convb, dtb, alog, None, None, n_lat_chunks, rev=True)
        ssd = _ssd_scan(xbc, dt, z, yb, convw, convb, dtb, alog, dskip, nw, n_lat_chunks, rev=False)
        attn = _attention(q, k, v, seq, n_ctx)
        xc = _mix_mlp(xc, ssd, attn, pool, mod_all[i], wos, woa, wop, pw, row(pool_scale[i]), row(norm2_w[i]),
                      w_mlp1[i].astype(BF16), w_mlp2[i].astype(BF16), row(final_norm_w), n_lat_tiles,
                      seq, n_ctx, final)
    return xc
```

```python
import functools
import math

import jax
import jax.numpy as jnp
import numpy as np
from jax import lax
from jax.experimental import pallas as pl
from jax.experimental.pallas import tpu as pltpu

F32 = jnp.float32
BF16 = jnp.bfloat16

D_MODEL = 1024
GRID_W = 64
EPS = 1e-6

SSD_HEADS = 6
SSD_HEAD_DIM = 64
SSD_GROUPS = 2
SSD_STATE = 128
SSD_CONV = 4
SSD_CHUNK = 128
HEADS_PER_GROUP = SSD_HEADS // SSD_GROUPS
GROUP_DIM = HEADS_PER_GROUP * SSD_HEAD_DIM
GROUP_PAD = 256
SSD_INNER = SSD_HEADS * SSD_HEAD_DIM
SSD_PAD = SSD_GROUPS * GROUP_PAD
XBC_DIM = SSD_INNER + 2 * SSD_GROUPS * SSD_STATE
XBC_PAD = SSD_PAD + 2 * SSD_GROUPS * SSD_STATE

MLA_HEADS = 6
Q_LORA = 256
KV_LORA = 256
QK_NOPE = 64
QK_ROPE = 32
V_HEAD = 64
QK_DIM = QK_NOPE + QK_ROPE
MLA_OUT = MLA_HEADS * V_HEAD
ROPE_THETA = 10000.0
ROPE_PAIRS = QK_ROPE // 4
HEAD_SLAB = 128

POOL_WINDOWS = (2, 4, 8, 16)
POOL_GROUP_DIM = 64
POOL_DIM = len(POOL_WINDOWS) * POOL_GROUP_DIM
POOL_HALO = 8
CONV_HALO = 8

D_FF = 4 * D_MODEL
FF_CHUNK = 512

OFF_Z = 0
OFF_XBC = OFF_Z + SSD_INNER
OFF_DT = OFF_XBC + XBC_DIM
OFF_QA = OFF_DT + 2 * SSD_HEADS
OFF_KVA = OFF_QA + Q_LORA
OFF_KROPE = OFF_KVA + KV_LORA
OFF_POOL = OFF_KROPE + QK_ROPE
IN_COLS = OFF_POOL + POOL_DIM

P_Z = 0
P_XBC = P_Z + SSD_PAD
P_QA = P_XBC + XBC_PAD
P_KVA = P_QA + Q_LORA
P_POOL = P_KVA + KV_LORA
P_DT = P_POOL + POOL_DIM
P_KR = P_DT + 128
P_COLS = P_KR + HEAD_SLAB

TM = 256
ATT_TQ = 256
ATT_TQ_LATENT = 512
ATT_TK = 512
VT_ROWS = 80
NEG_BIG = -1e30
VMEM_LIMIT = 56 * 1024 * 1024
LOG2E = math.log2(math.e)


def _pad_group_index(base):
    idx = np.full((SSD_PAD,), -1, np.int64)
    for g in range(SSD_GROUPS):
        idx[g * GROUP_PAD:g * GROUP_PAD + GROUP_DIM] = base + g * GROUP_DIM + np.arange(GROUP_DIM)
    return idx


def _rope_perm():
    src = np.zeros((QK_ROPE,), np.int64)
    for half in range(2):
        for axis in range(2):
            for p in range(ROPE_PAIRS):
                src[half * 16 + axis * 8 + p] = axis * 16 + half * 8 + p
    return src


def _win_index():
    idx = np.full((P_COLS,), -1, np.int64)
    idx[P_Z:P_Z + SSD_PAD] = _pad_group_index(OFF_Z)
    idx[P_XBC:P_XBC + SSD_PAD] = _pad_group_index(OFF_XBC)
    nbc = 2 * SSD_GROUPS * SSD_STATE
    idx[P_XBC + SSD_PAD:P_XBC + SSD_PAD + nbc] = OFF_XBC + SSD_INNER + np.arange(nbc)
    idx[P_QA:P_QA + Q_LORA] = OFF_QA + np.arange(Q_LORA)
    idx[P_KVA:P_KVA + KV_LORA] = OFF_KVA + np.arange(KV_LORA)
    idx[P_POOL:P_POOL + POOL_DIM] = OFF_POOL + np.arange(POOL_DIM)
    idx[P_DT:P_DT + 2 * SSD_HEADS] = OFF_DT + np.arange(2 * SSD_HEADS)
    idx[P_KR + QK_NOPE:P_KR + QK_NOPE + QK_ROPE] = OFF_KROPE + _rope_perm()
    return idx


def _wq_index():
    idx = np.full((MLA_HEADS * HEAD_SLAB,), -1, np.int64)
    perm = _rope_perm()
    for h in range(MLA_HEADS):
        idx[h * HEAD_SLAB:h * HEAD_SLAB + QK_NOPE] = h * QK_DIM + np.arange(QK_NOPE)
        idx[h * HEAD_SLAB + QK_NOPE:h * HEAD_SLAB + QK_DIM] = h * QK_DIM + QK_NOPE + perm
    return idx


def _wkv_index():
    nk = MLA_HEADS * HEAD_SLAB
    idx = np.full((nk + MLA_OUT,), -1, np.int64)
    per_head = QK_NOPE + V_HEAD
    for h in range(MLA_HEADS):
        idx[h * HEAD_SLAB:h * HEAD_SLAB + QK_NOPE] = h * per_head + np.arange(QK_NOPE)
        idx[nk + h * V_HEAD:nk + (h + 1) * V_HEAD] = h * per_head + QK_NOPE + np.arange(V_HEAD)
    return idx


def _remap_cols(w, idx):
    take = jnp.take(w, jnp.asarray(np.maximum(idx, 0)), axis=-1)
    return jnp.where(jnp.asarray(idx >= 0), take, jnp.zeros((), w.dtype))


def _rope_tables(seq, ctx):
    t = np.arange(seq)
    row = jnp.asarray((t // GRID_W).astype(np.float32))
    col = jnp.asarray((t % GRID_W).astype(np.float32))
    inv_freq = ROPE_THETA ** (-jnp.arange(ROPE_PAIRS, dtype=F32) / ROPE_PAIRS)
    ang = jnp.concatenate([row[:, None] * inv_freq, col[:, None] * inv_freq], axis=1)
    cos16, sin16 = jnp.cos(ang), jnp.sin(ang)
    ones = jnp.ones((seq, QK_NOPE), F32)
    zeros = jnp.zeros((seq, QK_NOPE), F32)
    z16 = jnp.zeros((seq, 16), F32)
    tail1 = jnp.ones((seq, HEAD_SLAB - QK_DIM), F32)
    tail0 = jnp.zeros((seq, HEAD_SLAB - QK_DIM), F32)
    c_tab = jnp.concatenate([ones, cos16, cos16, tail1], axis=1)
    s1_tab = jnp.concatenate([zeros, -sin16, z16, tail0], axis=1)
    s2_tab = jnp.concatenate([zeros, z16, sin16, tail0], axis=1)
    c_tab = jnp.concatenate([c_tab, jnp.ones((ctx, HEAD_SLAB), F32)], axis=0)
    s1_tab = jnp.concatenate([s1_tab, jnp.zeros((ctx, HEAD_SLAB), F32)], axis=0)
    s2_tab = jnp.concatenate([s2_tab, jnp.zeros((ctx, HEAD_SLAB), F32)], axis=0)
    k_tabs = jnp.stack([c_tab, s1_tab, s2_tab])
    qscale = (QK_DIM ** -0.5) * LOG2E
    return jnp.concatenate([k_tabs, k_tabs * qscale], axis=0)


def _rms(u, w_row):
    ms = jnp.mean(u * u, axis=-1, keepdims=True)
    return u * lax.rsqrt(ms + EPS) * w_row


def _silu(u):
    return u * (1.0 / (1.0 + jnp.exp(-u)))


def _softplus(u):
    return jnp.maximum(u, 0.0) + jnp.log1p(jnp.exp(-jnp.abs(u)))


def _mod_kernel(cond_ref, w_ref, b_ref, o_ref):
    cnd = cond_ref[...]
    o_ref[0] = jnp.dot(_silu(cnd), w_ref[0], preferred_element_type=F32,
                       precision=lax.Precision.HIGHEST) + b_ref[0]


def _modulation(cond, mod_w, mod_b):
    depth = mod_w.shape[0]
    r = cond.shape[0]
    out = pl.pallas_call(
        _mod_kernel,
        grid=(depth, 6),
        in_specs=[pl.BlockSpec((r, D_MODEL), lambda d, j: (0, 0)),
                  pl.BlockSpec((1, D_MODEL, D_MODEL), lambda d, j: (d, 0, j)),
                  pl.BlockSpec((1, 1, D_MODEL), lambda d, j: (d, 0, j))],
        out_specs=pl.BlockSpec((1, r, D_MODEL), lambda d, j: (d, 0, j)),
        out_shape=jax.ShapeDtypeStruct((depth, r, 6 * D_MODEL), F32),
        name="adaln_modulation",
    )(cond, mod_w, mod_b.reshape(depth, 1, 6 * D_MODEL))
    return out.reshape(depth, r, 6, D_MODEL)


def _rope(t, c, s1, s2):
    return t * c + pltpu.roll(t, HEAD_SLAB - 16, 1) * s1 + pltpu.roll(t, 16, 1) * s2


def _inproj_kernel(x_ref, mod_ref, n1_ref, win_ref, qan_ref, kvan_ref, wq_ref, wkv_ref, rope_ref,
                   z_ref, xbc_ref, dt_ref, pool_ref, q_ref, k_ref, vt_ref):
    h = _rms(x_ref[0], n1_ref[...])
    h = h * (1.0 + mod_ref[0, 1:2, :]) + mod_ref[0, 0:1, :]
    hb = h.astype(BF16)

    def proj(lo, width):
        return jnp.dot(hb, win_ref[:, lo:lo + width], preferred_element_type=F32)

    z_ref[0] = proj(P_Z, SSD_PAD)
    xbc_ref[0] = proj(P_XBC, XBC_PAD)
    pool_ref[0] = proj(P_POOL, POOL_DIM)
    dt_ref[0] = proj(P_DT, 128)

    cq = _rms(proj(P_QA, Q_LORA), qan_ref[...]).astype(BF16)
    ckv = _rms(proj(P_KVA, KV_LORA), kvan_ref[...]).astype(BF16)
    q = jnp.dot(cq, wq_ref[...], preferred_element_type=F32)
    kv = jnp.dot(ckv, wkv_ref[...], preferred_element_type=F32)
    k_rope = _rope(proj(P_KR, HEAD_SLAB), rope_ref[0], rope_ref[1], rope_ref[2])
    for hd in range(MLA_HEADS):
        sl = slice(hd * HEAD_SLAB, (hd + 1) * HEAD_SLAB)
        q_ref[0, :, sl] = _rope(q[:, sl], rope_ref[3], rope_ref[4], rope_ref[5]).astype(BF16)
        k_ref[0, :, sl] = (kv[:, sl] + k_rope).astype(BF16)
    nk = MLA_HEADS * HEAD_SLAB
    extra = VT_ROWS - V_HEAD
    ones_rows = (lax.broadcasted_iota(jnp.int32, (extra, TM), 0) == 0).astype(BF16)
    for pair in range(MLA_HEADS // 2):
        vt = kv[:, nk + pair * HEAD_SLAB:nk + (pair + 1) * HEAD_SLAB].T.astype(BF16)
        for sub in range(2):
            base = (2 * pair + sub) * VT_ROWS
            vt_ref[0, base:base + V_HEAD, :] = vt[sub * V_HEAD:(sub + 1) * V_HEAD, :]
            vt_ref[0, base + V_HEAD:base + VT_ROWS, :] = ones_rows


def _inproj(xc, mod, n1, win, qan, kvan, wq, wkv, rope, n_lat_tiles):
    bsz, t, _ = xc.shape
    nt = t // TM
    ctx_row = mod.shape[0] - 1

    def tok(width):
        return pl.BlockSpec((1, TM, width), lambda b, i: (b, i, 0))

    def const(shape):
        return pl.BlockSpec(shape, lambda b, i: (0,) * len(shape))

    mod_spec = pl.BlockSpec((1, 6, D_MODEL), lambda b, i: (jnp.where(i >= n_lat_tiles, ctx_row, b), 0, 0))
    nh = MLA_HEADS * HEAD_SLAB
    outs = [(SSD_PAD, F32), (XBC_PAD, F32), (128, F32), (POOL_DIM, F32), (nh, BF16), (nh, BF16)]
    vt_rows = MLA_HEADS * VT_ROWS
    return pl.pallas_call(
        _inproj_kernel,
        grid=(bsz, nt),
        in_specs=[tok(D_MODEL), mod_spec, const((1, D_MODEL)), const((D_MODEL, P_COLS)),
                  const((1, Q_LORA)), const((1, KV_LORA)), const((Q_LORA, nh)), const((KV_LORA, nh + MLA_OUT)),
                  pl.BlockSpec((6, TM, HEAD_SLAB), lambda b, i: (0, i, 0))],
        out_specs=[tok(w) for w, _ in outs] + [pl.BlockSpec((1, vt_rows, TM), lambda b, i: (b, 0, i))],
        out_shape=[jax.ShapeDtypeStruct((bsz, t, w), dt) for w, dt in outs]
                  + [jax.ShapeDtypeStruct((bsz, vt_rows, t), BF16)],
        compiler_params=pltpu.CompilerParams(dimension_semantics=("parallel", "parallel"),
                                             vmem_limit_bytes=VMEM_LIMIT),
        name="inproj_mla",
    )(xc, mod, n1, win, qan, kvan, wq, wkv, rope)


def _expand_heads(cols, first_lane):
    n = cols.shape[0]
    lane = lax.broadcasted_iota(jnp.int32, (n, 128), 1)
    parts = []
    for g in range(SSD_GROUPS):
        b = [jnp.broadcast_to(cols[:, first_lane + HEADS_PER_GROUP * g + i:first_lane + HEADS_PER_GROUP * g + i + 1],
                              (n, 128)) for i in range(HEADS_PER_GROUP)]
        parts.append(jnp.where(lane < SSD_HEAD_DIM, b[0], b[1]))
        parts.append(b[2])
    return jnp.concatenate(parts, axis=1)


def _ssd_kernel(*refs, rev, n_lat_chunks, n_chunks):
    if rev:
        (xbc_ref, xprev_ref, xnext_ref, dt_ref, convw_ref, convb_ref, dtb_ref, alog_ref,
         out_ref, ext_scr, h_scr) = refs
    else:
        (xbc_ref, xprev_ref, xnext_ref, dt_ref, z_ref, yb_ref, convw_ref, convb_ref, dtb_ref, alog_ref,
         dskip_ref, nw_ref, out_ref, ext_scr, h_scr) = refs
    L = SSD_CHUNK
    j = pl.program_id(1)
    c = _ssd_chunk(j, rev, n_lat_chunks, n_chunks)

    @pl.when(j == 0)
    def _():
        h_scr[...] = jnp.zeros_like(h_scr)

    seg_first = jnp.logical_or(c == 0, c == n_lat_chunks)
    seg_last = jnp.logical_or(c == n_lat_chunks - 1, c == n_chunks - 1)
    ext_scr[0:CONV_HALO, :] = xprev_ref[0] * jnp.where(seg_first, 0.0, 1.0)
    ext_scr[CONV_HALO:CONV_HALO + L, :] = xbc_ref[0]
    ext_scr[CONV_HALO + L:CONV_HALO + L + CONV_HALO, :] = xnext_ref[0] * jnp.where(seg_last, 0.0, 1.0)
    u = convb_ref[...]
    for tap in range(SSD_CONV):
        lo = CONV_HALO - 1 + tap
        u = u + ext_scr[lo:lo + L, :] * convw_ref[tap:tap + 1, :]
    u = _silu(u)
    xs = u[:, :SSD_PAD]
    bmat = u[:, SSD_PAD:SSD_PAD + SSD_GROUPS * SSD_STATE].astype(BF16)
    cmat = u[:, SSD_PAD + SSD_GROUPS * SSD_STATE:].astype(BF16)

    first_lane = SSD_HEADS if rev else 0
    dtv = _softplus(dt_ref[0] + dtb_ref[...])
    adt = dtv * (-jnp.exp(alog_ref[...]))
    ri = lax.broadcasted_iota(jnp.int32, (L, L), 0)
    ci = lax.broadcasted_iota(jnp.int32, (L, L), 1)
    causal = (ci >= ri) if rev else (ci <= ri)
    cs_col = jnp.dot(causal.astype(F32), adt, preferred_element_type=F32,
                     precision=lax.Precision.HIGHEST)
    cs_row = cs_col.T

    dt_e = _expand_heads(dtv, first_lane)
    cs_e = _expand_heads(cs_col, first_lane)
    end_row = 0 if rev else L - 1
    cs_end = cs_e[end_row:end_row + 1, :]
    xdt = xs * dt_e
    xdt_b = xdt.astype(BF16)
    xdd_b = (xdt * jnp.exp(cs_end - cs_e)).astype(BF16)
    in_decay = jnp.exp(cs_e)
    state_decay = jnp.exp(cs_end)

    lane_g = lax.broadcasted_iota(jnp.int32, (1, GROUP_PAD), 1)
    ys = []
    for g in range(SSD_GROUPS):
        gs = slice(g * GROUP_PAD, (g + 1) * GROUP_PAD)
        ns = slice(g * SSD_STATE, (g + 1) * SSD_STATE)
        cm_g, bm_g = cmat[:, ns], bmat[:, ns]
        cb = lax.dot_general(cm_g, bm_g, (((1,), (1,)), ((), ())), preferred_element_type=F32)
        x_g = xdt_b[:, gs]
        y_g = jnp.zeros((L, GROUP_PAD), F32)
        for i in range(HEADS_PER_GROUP):
            hl = first_lane + HEADS_PER_GROUP * g + i
            diff = cs_col[:, hl:hl + 1] - cs_row[hl:hl + 1, :]
            decay = jnp.exp(jnp.where(causal, diff, NEG_BIG))
            head_lanes = jnp.logical_and(lane_g >= i * SSD_HEAD_DIM, lane_g < (i + 1) * SSD_HEAD_DIM)
            x_h = jnp.where(head_lanes, x_g, jnp.zeros((), BF16))
            y_g = y_g + jnp.dot((cb * decay).astype(BF16), x_h, preferred_element_type=F32)
        h_t = h_scr[g]
        y_in = jnp.dot(cm_g, h_t.astype(BF16), preferred_element_type=F32)
        y_g = y_g + y_in * in_decay[:, gs]
        s_new = lax.dot_general(bm_g, xdd_b[:, gs], (((0,), (0,)), ((), ())), preferred_element_type=F32)
        h_scr[g] = h_t * state_decay[:, gs] + s_new
        ys.append(y_g)
    y = jnp.concatenate(ys, axis=1)

    if rev:
        out_ref[0] = y
    else:
        y = y + yb_ref[0] + xs * dskip_ref[...]
        gated = y * _silu(z_ref[0])
        outs = []
        for g in range(SSD_GROUPS):
            sl = gated[:, g * GROUP_PAD:(g + 1) * GROUP_PAD]
            ms = jnp.sum(sl * sl, axis=-1, keepdims=True) * (1.0 / GROUP_DIM)
            outs.append(sl * lax.rsqrt(ms + EPS))
        out_ref[0] = (jnp.concatenate(outs, axis=1) * nw_ref[...]).astype(BF16)


def _ssd_chunk(j, rev, n_lat_chunks, n_chunks):
    if rev:
        return n_chunks - 1 - j
    n_ctx = n_chunks - n_lat_chunks
    return jnp.where(j < n_ctx, n_lat_chunks + j, j - n_ctx)


def _ssd_scan(xbc, dt, z, yb, convw, convb, dtb, alog, dskip, nw, n_lat_chunks, rev):
    bsz, t, _ = xbc.shape
    n_chunks = t // SSD_CHUNK
    halo_blocks = t // CONV_HALO
    per = SSD_CHUNK // CONV_HALO
    chunk = functools.partial(_ssd_chunk, rev=rev, n_lat_chunks=n_lat_chunks, n_chunks=n_chunks)

    def tok(width):
        return pl.BlockSpec((1, SSD_CHUNK, width), lambda b, j: (b, chunk(j), 0))

    def const(shape):
        return pl.BlockSpec(shape, lambda b, j: (0,) * len(shape))

    prev_spec = pl.BlockSpec((1, CONV_HALO, XBC_PAD), lambda b, j: (b, jnp.maximum(chunk(j) * per - 1, 0), 0))
    next_spec = pl.BlockSpec((1, CONV_HALO, XBC_PAD),
                             lambda b, j: (b, jnp.minimum((chunk(j) + 1) * per, halo_blocks - 1), 0))
    params = [convw, convb, dtb, alog]
    param_specs = [const(p.shape) for p in params]
    if rev:
        args = [xbc, xbc, xbc, dt] + params
        in_specs = [tok(XBC_PAD), prev_spec, next_spec, tok(128)] + param_specs
        out_dtype = F32
    else:
        args = [xbc, xbc, xbc, dt, z, yb] + params + [dskip, nw]
        in_specs = ([tok(XBC_PAD), prev_spec, next_spec, tok(128), tok(SSD_PAD), tok(SSD_PAD)] + param_specs
                    + [const(dskip.shape), const(nw.shape)])
        out_dtype = BF16
    return pl.pallas_call(
        functools.partial(_ssd_kernel, rev=rev, n_lat_chunks=n_lat_chunks, n_chunks=n_chunks),
        grid=(bsz, n_chunks),
        in_specs=in_specs,
        out_specs=tok(SSD_PAD),
        out_shape=jax.ShapeDtypeStruct((bsz, t, SSD_PAD), out_dtype),
        scratch_shapes=[pltpu.VMEM((SSD_CHUNK + 2 * CONV_HALO, XBC_PAD), F32),
                        pltpu.VMEM((SSD_GROUPS, SSD_STATE, GROUP_PAD), F32)],
        compiler_params=pltpu.CompilerParams(dimension_semantics=("parallel", "arbitrary"),
                                             vmem_limit_bytes=VMEM_LIMIT),
        name="ssd_bwd" if rev else "ssd_fwd",
    )(*args)


def _attn_kernel(*refs):
    q_ref, k_ref, vt_ref = refs[:3]
    o_ref, s_scr = refs[-2:]
    tq = q_ref.shape[1]
    n_keys = k_ref.shape[1]
    chunks = [(lo, min(ATT_TK, n_keys - lo)) for lo in range(0, n_keys, ATT_TK)]
    mblk = max(8, min(ATT_TK, n_keys, (32 * 1024) // tq))

    def scores(hd, lo, size, m_acc):
        sl = slice(hd * HEAD_SLAB, (hd + 1) * HEAD_SLAB)
        s_c = lax.dot_general(k_ref[0, lo:lo + size, sl], q_ref[0, :, sl], (((1,), (1,)), ((), ())),
                              preferred_element_type=F32)
        s_scr[hd % 2, lo:lo + size, :] = s_c
        for r in range(0, size, mblk):
            blk = s_scr[hd % 2, lo + r:lo + r + mblk, :]
            m_acc = blk if m_acc is None else jnp.maximum(m_acc, blk)
        return m_acc

    def column_max(m_acc):
        rows = m_acc.shape[0]
        while rows > 8:
            rows //= 2
            m_acc = jnp.maximum(m_acc[0:rows], m_acc[rows:2 * rows])
        return jnp.max(m_acc, axis=0, keepdims=True)

    def exp_pv(hd, lo, size, m, acc):
        p_t = jnp.exp2(s_scr[hd % 2, lo:lo + size, :] - m).astype(BF16)
        return acc + jnp.dot(vt_ref[0, hd * VT_ROWS:(hd + 1) * VT_ROWS, lo:lo + size], p_t,
                             preferred_element_type=F32)

    def finish(acc):
        return acc[0:V_HEAD] * (1.0 / acc[V_HEAD:V_HEAD + 1])

    out_t = []
    m_prev = None
    for hd in range(MLA_HEADS):
        m_acc = None
        acc = jnp.zeros((VT_ROWS, tq), F32)
        for lo, size in chunks:
            m_acc = scores(hd, lo, size, m_acc)
            if hd > 0:
                acc = exp_pv(hd - 1, lo, size, m_prev, acc)
        if hd > 0:
            out_t.append(finish(acc))
        m_prev = column_max(m_acc)
    acc = jnp.zeros((VT_ROWS, tq), F32)
    for lo, size in chunks:
        acc = exp_pv(MLA_HEADS - 1, lo, size, m_prev, acc)
    out_t.append(finish(acc))
    for pair in range(MLA_HEADS // 2):
        o_ref[0, :, pair * HEAD_SLAB:(pair + 1) * HEAD_SLAB] = (
            jnp.concatenate(out_t[2 * pair:2 * pair + 2], axis=0).T.astype(BF16))


def _attention(q, k, vt, seq, ctx, prev_out=None):
    bsz, t, _ = q.shape
    latent = prev_out is None
    tq = ATT_TQ_LATENT if latent else ATT_TQ
    n_tiles = (seq if latent else ctx) // tq
    q_off = 0 if latent else seq // tq
    n_keys = t if latent else ctx
    key_blk = 0 if latent else seq // ctx
    nh = MLA_HEADS * HEAD_SLAB
    in_specs = [pl.BlockSpec((1, tq, nh), lambda b, i: (b, q_off + i, 0)),
                pl.BlockSpec((1, n_keys, nh), lambda b, i: (b, key_blk, 0)),
                pl.BlockSpec((1, MLA_HEADS * VT_ROWS, n_keys), lambda b, i: (b, 0, key_blk))]
    args = [q, k, vt]
    if not latent:
        in_specs.append(pl.BlockSpec(memory_space=pl.ANY))
        args.append(prev_out)
    return pl.pallas_call(
        _attn_kernel,
        grid=(bsz, n_tiles),
        in_specs=in_specs,
        out_specs=pl.BlockSpec((1, tq, MLA_OUT), lambda b, i: (b, q_off + i, 0)),
        out_shape=jax.ShapeDtypeStruct((bsz, t, MLA_OUT), BF16),
        scratch_shapes=[pltpu.VMEM((2, n_keys, tq), F32)],
        input_output_aliases={} if latent else {3: 0},
        compiler_params=pltpu.CompilerParams(dimension_semantics=("parallel", "arbitrary"),
                                             vmem_limit_bytes=VMEM_LIMIT),
        name="mla_attention" if latent else "mla_attention_ctx",
    )(*args)


def _pool_mix(pool_ref, pprev_ref, pnext_ref, ext_scr, pw_ref, ps_ref, tile, n_lat_tiles, n_tiles, seq, ctx):
    seg_first = jnp.logical_or(tile == 0, tile == n_lat_tiles)
    seg_last = jnp.logical_or(tile == n_lat_tiles - 1, tile == n_tiles - 1)
    h0 = POOL_HALO
    ext_scr[0:h0, :] = pprev_ref[0] * jnp.where(seg_first, 0.0, 1.0)
    ext_scr[h0:h0 + TM, :] = pool_ref[0]
    ext_scr[h0 + TM:h0 + TM + h0, :] = pnext_ref[0] * jnp.where(seg_last, 0.0, 1.0)

    is_ctx = tile >= n_lat_tiles
    seg_len = jnp.where(is_ctx, ctx, seq)
    pos = lax.broadcasted_iota(jnp.int32, (TM, 1), 0) + jnp.where(is_ctx, tile - n_lat_tiles, tile) * TM
    lane = lax.broadcasted_iota(jnp.int32, (1, 128), 1)
    low_half = lane < POOL_GROUP_DIM

    def inv_count(w):
        cnt = jnp.minimum(pos + (w - w // 2), seg_len) - jnp.maximum(pos - w // 2, 0)
        return 1.0 / cnt.astype(F32)

    def taps(slab, offsets):
        tot = None
        for k in offsets:
            piece = ext_scr[h0 + k:h0 + k + TM, slab * 128:(slab + 1) * 128]
            tot = piece if tot is None else tot + piece
        return tot

    outs = []
    for slab in range(2):
        w_small, w_big = POOL_WINDOWS[2 * slab], POOL_WINDOWS[2 * slab + 1]
        small = taps(slab, range(-(w_small // 2), w_small - w_small // 2))
        extra = [k for k in range(-(w_big // 2), w_big - w_big // 2)
                 if not -(w_small // 2) <= k < w_small - w_small // 2]
        big = small + taps(slab, extra)
        mean = jnp.where(low_half, small * inv_count(w_small), big * inv_count(w_big))
        outs.append(mean - ext_scr[h0:h0 + TM, slab * 128:(slab + 1) * 128])
    d = jnp.concatenate(outs, axis=1).astype(BF16)
    return jnp.dot(d, pw_ref[...], preferred_element_type=F32) * ps_ref[...]


def _mix_mlp_kernel(*refs, final, n_lat_tiles, n_tiles, seq, ctx):
    if final:
        (x_ref, ssd_ref, attn_ref, pool_ref, pprev_ref, pnext_ref, mod_ref, wos_ref, woa_ref, wop_ref,
         pw_ref, ps_ref, n2_ref, w1_ref, w2_ref, fn_ref, o_ref, ext_scr) = refs
    else:
        (x_ref, ssd_ref, attn_ref, pool_ref, pprev_ref, pnext_ref, mod_ref, wos_ref, woa_ref, wop_ref,
         pw_ref, ps_ref, n2_ref, w1_ref, w2_ref, o_ref, ext_scr) = refs
    tile = pl.program_id(1)
    pool_y = _pool_mix(pool_ref, pprev_ref, pnext_ref, ext_scr, pw_ref, ps_ref, tile,
                       n_lat_tiles, n_tiles, seq, ctx)
    mix = (jnp.dot(ssd_ref[0], wos_ref[...], preferred_element_type=F32)
           + jnp.dot(attn_ref[0], woa_ref[...], preferred_element_type=F32)
           + jnp.dot(pool_y.astype(BF16), wop_ref[...], preferred_element_type=F32))
    x1 = x_ref[0] + mod_ref[0, 2:3, :] * mix
    h = _rms(x1, n2_ref[...])
    hb = (h * (1.0 + mod_ref[0, 4:5, :]) + mod_ref[0, 3:4, :]).astype(BF16)
    acc = jnp.zeros((TM, D_MODEL), F32)
    for lo in range(0, D_FF, FF_CHUNK):
        a = jnp.maximum(jnp.dot(hb, w1_ref[:, lo:lo + FF_CHUNK], preferred_element_type=F32), 0.0)
        acc = acc + jnp.dot((a * a).astype(BF16), w2_ref[lo:lo + FF_CHUNK, :], preferred_element_type=F32)
    x2 = x1 + mod_ref[0, 5:6, :] * acc
    if final:
        x2 = _rms(x2, fn_ref[...])
    o_ref[0] = x2


def _mix_mlp(xc, ssd, attn, pool, mod, wos, woa, wop, pw, ps, n2, w1, w2, fn, n_lat_tiles, seq, ctx, final):
    bsz, t, _ = xc.shape
    n_tiles = t // TM
    out_tiles = n_lat_tiles if final else n_tiles
    halo_blocks = t // POOL_HALO
    per = TM // POOL_HALO
    ctx_row = mod.shape[0] - 1

    def tok(width):
        return pl.BlockSpec((1, TM, width), lambda b, i: (b, i, 0))

    def const(shape):
        return pl.BlockSpec(shape, lambda b, i: (0,) * len(shape), pipeline_mode=pl.Buffered(1))

    prev_spec = pl.BlockSpec((1, POOL_HALO, POOL_DIM), lambda b, i: (b, jnp.maximum(i * per - 1, 0), 0))
    next_spec = pl.BlockSpec((1, POOL_HALO, POOL_DIM),
                             lambda b, i: (b, jnp.minimum((i + 1) * per, halo_blocks - 1), 0))
    mod_spec = pl.BlockSpec((1, 6, D_MODEL), lambda b, i: (jnp.where(i >= n_lat_tiles, ctx_row, b), 0, 0))
    weights = [wos, woa, wop, pw, ps, n2, w1, w2] + ([fn] if final else [])
    return pl.pallas_call(
        functools.partial(_mix_mlp_kernel, final=final, n_lat_tiles=n_lat_tiles, n_tiles=n_tiles,
                          seq=seq, ctx=ctx),
        grid=(bsz, out_tiles),
        in_specs=[tok(D_MODEL), tok(SSD_PAD), tok(MLA_OUT), tok(POOL_DIM), prev_spec, next_spec, mod_spec]
                 + [const(w.shape) for w in weights],
        out_specs=tok(D_MODEL),
        out_shape=jax.ShapeDtypeStruct((bsz, out_tiles * TM, D_MODEL), F32),
        scratch_shapes=[pltpu.VMEM((TM + 2 * POOL_HALO, POOL_DIM), F32)],
        compiler_params=pltpu.CompilerParams(dimension_semantics=("parallel", "parallel"),
                                             vmem_limit_bytes=VMEM_LIMIT),
        name="mix_mlp_final" if final else "mix_mlp",
    )(xc, ssd, attn, pool, pool, pool, mod, *weights)


def kernel(x, c, ctx, c_ctx, mod_w, mod_b, norm1_w, norm2_w, w_in, conv_w, conv_b, dt_bias, a_log, ssd_d,
           ssd_norm_w, q_a_norm_w, w_q_b, kv_a_norm_w, w_kv_b, pool_w, pool_scale, w_out, w_mlp1, w_mlp2,
           final_norm_w):
    bsz, seq, _ = x.shape
    n_ctx = ctx.shape[1]
    depth = mod_w.shape[0]
    assert seq % TM == 0 and n_ctx % TM == 0 and seq % GRID_W == 0
    assert seq % SSD_CHUNK == 0 and n_ctx % SSD_CHUNK == 0
    assert seq % ATT_TQ_LATENT == 0 and n_ctx % ATT_TQ == 0 and seq % n_ctx == 0
    n_lat_tiles = seq // TM
    n_lat_chunks = seq // SSD_CHUNK

    xc = jnp.concatenate([x, ctx], axis=1)
    cond_rows = -(-(bsz + 1) // 8) * 8
    cond = jnp.zeros((cond_rows, D_MODEL), F32).at[:bsz].set(c).at[bsz].set(c_ctx)
    mod_all = _modulation(cond, mod_w, mod_b)[:, :bsz + 1]
    rope = _rope_tables(seq, n_ctx)

    win_idx, wq_idx, wkv_idx = _win_index(), _wq_index(), _wkv_index()
    pad_idx = _pad_group_index(0)
    xbc_idx = np.concatenate([pad_idx, SSD_INNER + np.arange(XBC_DIM - SSD_INNER)])
    row = lambda v: v.reshape(1, -1)

    for i in range(depth):
        final = i == depth - 1
        win = _remap_cols(w_in[i], win_idx).astype(BF16)
        wq = _remap_cols(w_q_b[i], wq_idx).astype(BF16)
        wkv = _remap_cols(w_kv_b[i], wkv_idx).astype(BF16)
        convw = _remap_cols(conv_w[i], xbc_idx)
        convb = row(_remap_cols(conv_b[i], xbc_idx))
        dtb = row(jnp.pad(dt_bias[i].reshape(-1), (0, 128 - 2 * SSD_HEADS)))
        alog = row(jnp.pad(a_log[i].reshape(-1), (0, 128 - 2 * SSD_HEADS)))
        dskip = row(_remap_cols(jnp.repeat(ssd_d[i], SSD_HEAD_DIM), pad_idx))
        nw = row(_remap_cols(ssd_norm_w[i], pad_idx))
        wo = w_out[i].astype(BF16)
        wos = _remap_cols(wo[:SSD_INNER].T, pad_idx).T
        woa = wo[SSD_INNER:SSD_INNER + MLA_OUT]
        wop = wo[SSD_INNER + MLA_OUT:]
        pw = jax.scipy.linalg.block_diag(*[pool_w[i, g] for g in range(len(POOL_WINDOWS))]).astype(BF16)

        z, xbc, dt, pool, q, k, v = _inproj(xc, mod_all[i], row(norm1_w[i]), win, row(q_a_norm_w[i]),
                                             row(kv_a_norm_w[i]), wq, wkv, rope, n_lat_tiles)
        yb = _ssd_scan(xbc, dt, None, None, convw, convb, dtb, alog, None, None, n_lat_chunks, rev=True)
        ssd = _ssd_scan(xbc, dt, z, yb, convw, convb, dtb, alog, dskip, nw, n_lat_chunks, rev=False)
        attn = _attention(q, k, v, seq, n_ctx)
        if not final:
            attn = _attention(q, k, v, seq, n_ctx, prev_out=attn)
        xc = _mix_mlp(xc, ssd, attn, pool, mod_all[i], wos, woa, wop, pw, row(pool_scale[i]), row(norm2_w[i]),
                      w_mlp1[i].astype(BF16), w_mlp2[i].astype(BF16), row(final_norm_w), n_lat_tiles,
                      seq, n_ctx, final)
    return xc
```

```python
import functools
import math

import jax
import jax.numpy as jnp
import numpy as np
from jax import lax
from jax.experimental import pallas as pl
from jax.experimental.pallas import tpu as pltpu

F32 = jnp.float32
BF16 = jnp.bfloat16

D_MODEL = 1024
GRID_W = 64
EPS = 1e-6

SSD_HEADS = 6
SSD_HEAD_DIM = 64
SSD_GROUPS = 2
SSD_STATE = 128
SSD_CONV = 4
SSD_CHUNK = 128
HEADS_PER_GROUP = SSD_HEADS // SSD_GROUPS
GROUP_DIM = HEADS_PER_GROUP * SSD_HEAD_DIM
GROUP_PAD = 256
SSD_INNER = SSD_HEADS * SSD_HEAD_DIM
SSD_PAD = SSD_GROUPS * GROUP_PAD
XBC_DIM = SSD_INNER + 2 * SSD_GROUPS * SSD_STATE
XBC_PAD = SSD_PAD + 2 * SSD_GROUPS * SSD_STATE

MLA_HEADS = 6
Q_LORA = 256
KV_LORA = 256
QK_NOPE = 64
QK_ROPE = 32
V_HEAD = 64
QK_DIM = QK_NOPE + QK_ROPE
MLA_OUT = MLA_HEADS * V_HEAD
ROPE_THETA = 10000.0
ROPE_PAIRS = QK_ROPE // 4
HEAD_SLAB = 128

POOL_WINDOWS = (2, 4, 8, 16)
POOL_GROUP_DIM = 64
POOL_DIM = len(POOL_WINDOWS) * POOL_GROUP_DIM
POOL_HALO = 8
CONV_HALO = 8

D_FF = 4 * D_MODEL
FF_CHUNK = 512

OFF_Z = 0
OFF_XBC = OFF_Z + SSD_INNER
OFF_DT = OFF_XBC + XBC_DIM
OFF_QA = OFF_DT + 2 * SSD_HEADS
OFF_KVA = OFF_QA + Q_LORA
OFF_KROPE = OFF_KVA + KV_LORA
OFF_POOL = OFF_KROPE + QK_ROPE
IN_COLS = OFF_POOL + POOL_DIM

P_Z = 0
P_XBC = P_Z + SSD_PAD
P_QA = P_XBC + XBC_PAD
P_KVA = P_QA + Q_LORA
P_POOL = P_KVA + KV_LORA
P_DT = P_POOL + POOL_DIM
P_KR = P_DT + 128
P_COLS = P_KR + HEAD_SLAB

TM = 256
ATT_TQ = 256
ATT_TQ_LATENT = 512
ATT_TK = 512
VT_ROWS = 80
NEG_BIG = -1e30
VMEM_LIMIT = 56 * 1024 * 1024
LOG2E = math.log2(math.e)


def _pad_group_index(base):
    idx = np.full((SSD_PAD,), -1, np.int64)
    for g in range(SSD_GROUPS):
        idx[g * GROUP_PAD:g * GROUP_PAD + GROUP_DIM] = base + g * GROUP_DIM + np.arange(GROUP_DIM)
    return idx


def _rope_perm():
    src = np.zeros((QK_ROPE,), np.int64)
    for half in range(2):
        for axis in range(2):
            for p in range(ROPE_PAIRS):
                src[half * 16 + axis * 8 + p] = axis * 16 + half * 8 + p
    return src


def _win_index():
    idx = np.full((P_COLS,), -1, np.int64)
    idx[P_Z:P_Z + SSD_PAD] = _pad_group_index(OFF_Z)
    idx[P_XBC:P_XBC + SSD_PAD] = _pad_group_index(OFF_XBC)
    nbc = 2 * SSD_GROUPS * SSD_STATE
    idx[P_XBC + SSD_PAD:P_XBC + SSD_PAD + nbc] = OFF_XBC + SSD_INNER + np.arange(nbc)
    idx[P_QA:P_QA + Q_LORA] = OFF_QA + np.arange(Q_LORA)
    idx[P_KVA:P_KVA + KV_LORA] = OFF_KVA + np.arange(KV_LORA)
    idx[P_POOL:P_POOL + POOL_DIM] = OFF_POOL + np.arange(POOL_DIM)
    idx[P_DT:P_DT + 2 * SSD_HEADS] = OFF_DT + np.arange(2 * SSD_HEADS)
    idx[P_KR + QK_NOPE:P_KR + QK_NOPE + QK_ROPE] = OFF_KROPE + _rope_perm()
    return idx


def _wq_index():
    idx = np.full((MLA_HEADS * HEAD_SLAB,), -1, np.int64)
    perm = _rope_perm()
    for h in range(MLA_HEADS):
        idx[h * HEAD_SLAB:h * HEAD_SLAB + QK_NOPE] = h * QK_DIM + np.arange(QK_NOPE)
        idx[h * HEAD_SLAB + QK_NOPE:h * HEAD_SLAB + QK_DIM] = h * QK_DIM + QK_NOPE + perm
    return idx


def _wkv_index():
    nk = MLA_HEADS * HEAD_SLAB
    idx = np.full((nk + MLA_OUT,), -1, np.int64)
    per_head = QK_NOPE + V_HEAD
    for h in range(MLA_HEADS):
        idx[h * HEAD_SLAB:h * HEAD_SLAB + QK_NOPE] = h * per_head + np.arange(QK_NOPE)
        idx[nk + h * V_HEAD:nk + (h + 1) * V_HEAD] = h * per_head + QK_NOPE + np.arange(V_HEAD)
    return idx


def _remap_cols(w, idx):
    take = jnp.take(w, jnp.asarray(np.maximum(idx, 0)), axis=-1)
    return jnp.where(jnp.asarray(idx >= 0), take, jnp.zeros((), w.dtype))


def _rope_tables(seq, ctx):
    t = np.arange(seq)
    row = jnp.asarray((t // GRID_W).astype(np.float32))
    col = jnp.asarray((t % GRID_W).astype(np.float32))
    inv_freq = ROPE_THETA ** (-jnp.arange(ROPE_PAIRS, dtype=F32) / ROPE_PAIRS)
    ang = jnp.concatenate([row[:, None] * inv_freq, col[:, None] * inv_freq], axis=1)
    cos16, sin16 = jnp.cos(ang), jnp.sin(ang)
    ones = jnp.ones((seq, QK_NOPE), F32)
    zeros = jnp.zeros((seq, QK_NOPE), F32)
    z16 = jnp.zeros((seq, 16), F32)
    tail1 = jnp.ones((seq, HEAD_SLAB - QK_DIM), F32)
    tail0 = jnp.zeros((seq, HEAD_SLAB - QK_DIM), F32)
    c_tab = jnp.concatenate([ones, cos16, cos16, tail1], axis=1)
    s1_tab = jnp.concatenate([zeros, -sin16, z16, tail0], axis=1)
    s2_tab = jnp.concatenate([zeros, z16, sin16, tail0], axis=1)
    c_tab = jnp.concatenate([c_tab, jnp.ones((ctx, HEAD_SLAB), F32)], axis=0)
    s1_tab = jnp.concatenate([s1_tab, jnp.zeros((ctx, HEAD_SLAB), F32)], axis=0)
    s2_tab = jnp.concatenate([s2_tab, jnp.zeros((ctx, HEAD_SLAB), F32)], axis=0)
    k_tabs = jnp.stack([c_tab, s1_tab, s2_tab])
    qscale = (QK_DIM ** -0.5) * LOG2E
    return jnp.concatenate([k_tabs, k_tabs * qscale], axis=0)


def _rms(u, w_row):
    ms = jnp.mean(u * u, axis=-1, keepdims=True)
    return u * lax.rsqrt(ms + EPS) * w_row


def _silu(u):
    return u * (1.0 / (1.0 + jnp.exp(-u)))


def _softplus(u):
    return jnp.maximum(u, 0.0) + jnp.log1p(jnp.exp(-jnp.abs(u)))


def _mod_kernel(cond_ref, w_ref, b_ref, o_ref):
    cnd = cond_ref[...]
    o_ref[0] = jnp.dot(_silu(cnd), w_ref[0], preferred_element_type=F32,
                       precision=lax.Precision.HIGHEST) + b_ref[0]


def _modulation(cond, mod_w, mod_b):
    depth = mod_w.shape[0]
    r = cond.shape[0]
    out = pl.pallas_call(
        _mod_kernel,
        grid=(depth, 6),
        in_specs=[pl.BlockSpec((r, D_MODEL), lambda d, j: (0, 0)),
                  pl.BlockSpec((1, D_MODEL, D_MODEL), lambda d, j: (d, 0, j)),
                  pl.BlockSpec((1, 1, D_MODEL), lambda d, j: (d, 0, j))],
        out_specs=pl.BlockSpec((1, r, D_MODEL), lambda d, j: (d, 0, j)),
        out_shape=jax.ShapeDtypeStruct((depth, r, 6 * D_MODEL), F32),
        name="adaln_modulation",
    )(cond, mod_w, mod_b.reshape(depth, 1, 6 * D_MODEL))
    return out.reshape(depth, r, 6, D_MODEL)


def _rope(t, c, s1, s2):
    return t * c + pltpu.roll(t, HEAD_SLAB - 16, 1) * s1 + pltpu.roll(t, 16, 1) * s2


def _token_tile(x_ref, ctx_ref, tile, n_lat_tiles):
    if ctx_ref is None:
        return x_ref[0]
    return jnp.where(tile >= n_lat_tiles, ctx_ref[0], x_ref[0])


def _inproj_kernel(*refs, n_lat_tiles, n_tiles, split_ctx):
    x_ref, ctx_ref = (refs[0], refs[1]) if split_ctx else (refs[0], None)
    (xprev_ref, xnext_ref, mod_ref, n1_ref, win_ref, convw_ref, convb_ref, qan_ref, kvan_ref, wq_ref, wkv_ref,
     rope_ref, z_ref, xs_ref, bc_ref, dt_ref, pool_ref, q_ref, k_ref, vt_ref, ext_scr) = refs[2 if split_ctx else 1:]

    def norm_mod(x):
        return (_rms(x, n1_ref[...]) * (1.0 + mod_ref[0, 1:2, :]) + mod_ref[0, 0:1, :]).astype(BF16)

    hb = norm_mod(_token_tile(x_ref, ctx_ref, pl.program_id(1), n_lat_tiles))

    def proj(lo, width):
        return jnp.dot(hb, win_ref[:, lo:lo + width], preferred_element_type=F32)

    z_ref[0] = proj(P_Z, SSD_PAD)
    pool_ref[0] = proj(P_POOL, POOL_DIM)

    tile = pl.program_id(1)
    seg_first = jnp.logical_or(tile == 0, tile == n_lat_tiles)
    seg_last = jnp.logical_or(tile == n_lat_tiles - 1, tile == n_tiles - 1)
    halo = norm_mod(jnp.concatenate([xprev_ref[0], xnext_ref[0]], axis=0))
    xbc = jnp.dot(jnp.concatenate([hb, halo], axis=0), win_ref[:, P_XBC:P_XBC + XBC_PAD],
                  preferred_element_type=F32)
    ext_scr[0:CONV_HALO, :] = xbc[TM:TM + CONV_HALO] * jnp.where(seg_first, 0.0, 1.0)
    ext_scr[CONV_HALO:CONV_HALO + TM, :] = xbc[0:TM]
    ext_scr[CONV_HALO + TM:, :] = xbc[TM + CONV_HALO:] * jnp.where(seg_last, 0.0, 1.0)
    u = convb_ref[...]
    for tap in range(SSD_CONV):
        lo = CONV_HALO - 1 + tap
        u = u + ext_scr[lo:lo + TM, :] * convw_ref[tap:tap + 1, :]
    u = _silu(u)
    xs_ref[0] = u[:, :SSD_PAD]
    bc_ref[0] = u[:, SSD_PAD:].astype(BF16)
    dt_ref[0] = proj(P_DT, 128)

    cq = _rms(proj(P_QA, Q_LORA), qan_ref[...]).astype(BF16)
    ckv = _rms(proj(P_KVA, KV_LORA), kvan_ref[...]).astype(BF16)
    q = jnp.dot(cq, wq_ref[...], preferred_element_type=F32)
    kv = jnp.dot(ckv, wkv_ref[...], preferred_element_type=F32)
    k_rope = _rope(proj(P_KR, HEAD_SLAB), rope_ref[0], rope_ref[1], rope_ref[2])
    for hd in range(MLA_HEADS):
        sl = slice(hd * HEAD_SLAB, (hd + 1) * HEAD_SLAB)
        q_ref[0, :, sl] = _rope(q[:, sl], rope_ref[3], rope_ref[4], rope_ref[5]).astype(BF16)
        k_ref[0, :, sl] = (kv[:, sl] + k_rope).astype(BF16)
    nk = MLA_HEADS * HEAD_SLAB
    extra = VT_ROWS - V_HEAD
    ones_rows = (lax.broadcasted_iota(jnp.int32, (extra, TM), 0) == 0).astype(BF16)
    for pair in range(MLA_HEADS // 2):
        vt = kv[:, nk + pair * HEAD_SLAB:nk + (pair + 1) * HEAD_SLAB].T.astype(BF16)
        for sub in range(2):
            base = (2 * pair + sub) * VT_ROWS
            vt_ref[0, base:base + V_HEAD, :] = vt[sub * V_HEAD:(sub + 1) * V_HEAD, :]
            vt_ref[0, base + V_HEAD:base + VT_ROWS, :] = ones_rows


def _inproj(xc, ctx_in, mod, n1, win, convw, convb, qan, kvan, wq, wkv, rope, n_lat_tiles):
    bsz = xc.shape[0]
    split_ctx = ctx_in is not None
    t = xc.shape[1] + (ctx_in.shape[1] if split_ctx else 0)
    nt = t // TM
    ctx_row = mod.shape[0] - 1
    halo_blocks = xc.shape[1] // CONV_HALO
    last_x_tile = xc.shape[1] // TM - 1
    per = TM // CONV_HALO

    def tok(width):
        return pl.BlockSpec((1, TM, width), lambda b, i: (b, i, 0))

    x_specs = [pl.BlockSpec((1, TM, D_MODEL), lambda b, i: (b, jnp.minimum(i, last_x_tile), 0))]
    x_args = [xc]
    if split_ctx:
        x_specs.append(pl.BlockSpec((1, TM, D_MODEL), lambda b, i: (b, 0, 0)))
        x_args.append(ctx_in)

    def const(shape):
        return pl.BlockSpec(shape, lambda b, i: (0,) * len(shape))

    prev_spec = pl.BlockSpec((1, CONV_HALO, D_MODEL), lambda b, i: (b, jnp.maximum(i * per - 1, 0), 0))
    next_spec = pl.BlockSpec((1, CONV_HALO, D_MODEL),
                             lambda b, i: (b, jnp.minimum((i + 1) * per, halo_blocks - 1), 0))
    mod_spec = pl.BlockSpec((1, 6, D_MODEL), lambda b, i: (jnp.where(i >= n_lat_tiles, ctx_row, b), 0, 0))
    nh = MLA_HEADS * HEAD_SLAB
    nbc = 2 * SSD_GROUPS * SSD_STATE
    outs = [(SSD_PAD, F32), (SSD_PAD, F32), (nbc, BF16), (128, F32), (POOL_DIM, F32), (nh, BF16), (nh, BF16)]
    vt_rows = MLA_HEADS * VT_ROWS
    return pl.pallas_call(
        functools.partial(_inproj_kernel, n_lat_tiles=n_lat_tiles, n_tiles=nt, split_ctx=split_ctx),
        grid=(bsz, nt),
        in_specs=x_specs + [prev_spec, next_spec, mod_spec, const((1, D_MODEL)), const((D_MODEL, P_COLS)),
                  const(convw.shape), const(convb.shape),
                  const((1, Q_LORA)), const((1, KV_LORA)), const((Q_LORA, nh)), const((KV_LORA, nh + MLA_OUT)),
                  pl.BlockSpec((6, TM, HEAD_SLAB), lambda b, i: (0, i, 0))],
        out_specs=[tok(w) for w, _ in outs] + [pl.BlockSpec((1, vt_rows, TM), lambda b, i: (b, 0, i))],
        out_shape=[jax.ShapeDtypeStruct((bsz, t, w), dt) for w, dt in outs]
                  + [jax.ShapeDtypeStruct((bsz, vt_rows, t), BF16)],
        scratch_shapes=[pltpu.VMEM((TM + 2 * CONV_HALO, XBC_PAD), F32)],
        compiler_params=pltpu.CompilerParams(dimension_semantics=("parallel", "parallel"),
                                             vmem_limit_bytes=VMEM_LIMIT),
        name="inproj_mla",
    )(*x_args, xc, xc, mod, n1, win, convw, convb, qan, kvan, wq, wkv, rope)


def _expand_heads(cols, first_lane):
    n = cols.shape[0]
    lane = lax.broadcasted_iota(jnp.int32, (n, 128), 1)
    parts = []
    for g in range(SSD_GROUPS):
        b = [jnp.broadcast_to(cols[:, first_lane + HEADS_PER_GROUP * g + i:first_lane + HEADS_PER_GROUP * g + i + 1],
                              (n, 128)) for i in range(HEADS_PER_GROUP)]
        parts.append(jnp.where(lane < SSD_HEAD_DIM, b[0], b[1]))
        parts.append(b[2])
    return jnp.concatenate(parts, axis=1)


def _ssd_kernel(*refs, rev, n_lat_chunks, n_chunks):
    if rev:
        xs_ref, bc_ref, dt_ref, dtb_ref, alog_ref, out_ref, h_scr = refs
    else:
        xs_ref, bc_ref, dt_ref, z_ref, yb_ref, dtb_ref, alog_ref, dskip_ref, nw_ref, out_ref, h_scr = refs
    L = SSD_CHUNK
    j = pl.program_id(1)

    @pl.when(j == 0)
    def _():
        h_scr[...] = jnp.zeros_like(h_scr)

    xs = xs_ref[0]
    bmat = bc_ref[0, :, 0:SSD_GROUPS * SSD_STATE]
    cmat = bc_ref[0, :, SSD_GROUPS * SSD_STATE:]

    first_lane = SSD_HEADS if rev else 0
    dtv = _softplus(dt_ref[0] + dtb_ref[...])
    adt = dtv * (-jnp.exp(alog_ref[...]))
    ri = lax.broadcasted_iota(jnp.int32, (L, L), 0)
    ci = lax.broadcasted_iota(jnp.int32, (L, L), 1)
    causal = (ci >= ri) if rev else (ci <= ri)
    cs_col = jnp.dot(causal.astype(F32), adt, preferred_element_type=F32,
                     precision=lax.Precision.HIGHEST)
    cs_row = cs_col.T

    dt_e = _expand_heads(dtv, first_lane)
    cs_e = _expand_heads(cs_col, first_lane)
    end_row = 0 if rev else L - 1
    cs_end = cs_e[end_row:end_row + 1, :]
    xdt = xs * dt_e
    xdt_b = xdt.astype(BF16)
    xdd_b = (xdt * jnp.exp(cs_end - cs_e)).astype(BF16)
    in_decay = jnp.exp(cs_e)
    state_decay = jnp.exp(cs_end)

    lane_g = lax.broadcasted_iota(jnp.int32, (1, GROUP_PAD), 1)
    ys = []
    for g in range(SSD_GROUPS):
        gs = slice(g * GROUP_PAD, (g + 1) * GROUP_PAD)
        ns = slice(g * SSD_STATE, (g + 1) * SSD_STATE)
        cm_g, bm_g = cmat[:, ns], bmat[:, ns]
        cb = lax.dot_general(cm_g, bm_g, (((1,), (1,)), ((), ())), preferred_element_type=F32)
        x_g = xdt_b[:, gs]
        y_g = jnp.zeros((L, GROUP_PAD), F32)
        for i in range(HEADS_PER_GROUP):
            hl = first_lane + HEADS_PER_GROUP * g + i
            diff = cs_col[:, hl:hl + 1] - cs_row[hl:hl + 1, :]
            decay = jnp.exp(jnp.where(causal, diff, NEG_BIG))
            head_lanes = jnp.logical_and(lane_g >= i * SSD_HEAD_DIM, lane_g < (i + 1) * SSD_HEAD_DIM)
            x_h = jnp.where(head_lanes, x_g, jnp.zeros((), BF16))
            y_g = y_g + jnp.dot((cb * decay).astype(BF16), x_h, preferred_element_type=F32)
        h_t = h_scr[g]
        y_in = jnp.dot(cm_g, h_t.astype(BF16), preferred_element_type=F32)
        y_g = y_g + y_in * in_decay[:, gs]
        s_new = lax.dot_general(bm_g, xdd_b[:, gs], (((0,), (0,)), ((), ())), preferred_element_type=F32)
        h_scr[g] = h_t * state_decay[:, gs] + s_new
        ys.append(y_g)
    y = jnp.concatenate(ys, axis=1)

    if rev:
        out_ref[0] = y
    else:
        y = y + yb_ref[0] + xs * dskip_ref[...]
        gated = y * _silu(z_ref[0])
        outs = []
        for g in range(SSD_GROUPS):
            sl = gated[:, g * GROUP_PAD:(g + 1) * GROUP_PAD]
            ms = jnp.sum(sl * sl, axis=-1, keepdims=True) * (1.0 / GROUP_DIM)
            outs.append(sl * lax.rsqrt(ms + EPS))
        out_ref[0] = (jnp.concatenate(outs, axis=1) * nw_ref[...]).astype(BF16)


def _ssd_chunk(j, rev, n_lat_chunks, n_chunks):
    if rev:
        return n_chunks - 1 - j
    n_ctx = n_chunks - n_lat_chunks
    return jnp.where(j < n_ctx, n_lat_chunks + j, j - n_ctx)


def _ssd_scan(xs, bc, dt, z, yb, dtb, alog, dskip, nw, n_lat_chunks, rev):
    bsz, t, _ = xs.shape
    n_chunks = t // SSD_CHUNK
    chunk = functools.partial(_ssd_chunk, rev=rev, n_lat_chunks=n_lat_chunks, n_chunks=n_chunks)

    def tok(width):
        return pl.BlockSpec((1, SSD_CHUNK, width), lambda b, j: (b, chunk(j), 0))

    def const(shape):
        return pl.BlockSpec(shape, lambda b, j: (0,) * len(shape))

    params = [dtb, alog]
    param_specs = [const(p.shape) for p in params]
    if rev:
        args = [xs, bc, dt] + params
        in_specs = [tok(SSD_PAD), tok(bc.shape[-1]), tok(128)] + param_specs
        out_dtype = F32
    else:
        args = [xs, bc, dt, z, yb] + params + [dskip, nw]
        in_specs = ([tok(SSD_PAD), tok(bc.shape[-1]), tok(128), tok(SSD_PAD), tok(SSD_PAD)] + param_specs
                    + [const(dskip.shape), const(nw.shape)])
        out_dtype = BF16
    return pl.pallas_call(
        functools.partial(_ssd_kernel, rev=rev, n_lat_chunks=n_lat_chunks, n_chunks=n_chunks),
        grid=(bsz, n_chunks),
        in_specs=in_specs,
        out_specs=tok(SSD_PAD),
        out_shape=jax.ShapeDtypeStruct((bsz, t, SSD_PAD), out_dtype),
        scratch_shapes=[pltpu.VMEM((SSD_GROUPS, SSD_STATE, GROUP_PAD), F32)],
        compiler_params=pltpu.CompilerParams(dimension_semantics=("parallel", "arbitrary"),
                                             vmem_limit_bytes=VMEM_LIMIT),
        name="ssd_bwd" if rev else "ssd_fwd",
    )(*args)


def _attn_kernel(*refs):
    q_ref, k_ref, vt_ref = refs[:3]
    o_ref, s_scr = refs[-2:]
    tq = q_ref.shape[1]
    n_keys = k_ref.shape[1]
    chunks = [(lo, min(ATT_TK, n_keys - lo)) for lo in range(0, n_keys, ATT_TK)]
    mblk = max(8, min(ATT_TK, n_keys, (8 * 1024) // tq))

    def scores(hd, lo, size, m_acc):
        sl = slice(hd * HEAD_SLAB, (hd + 1) * HEAD_SLAB)
        s_c = lax.dot_general(k_ref[0, lo:lo + size, sl], q_ref[0, :, sl], (((1,), (1,)), ((), ())),
                              preferred_element_type=F32)
        s_scr[hd % 2, lo:lo + size, :] = s_c
        for r in range(0, size, mblk):
            blk = s_scr[hd % 2, lo + r:lo + r + mblk, :]
            m_acc = blk if m_acc is None else jnp.maximum(m_acc, blk)
        return m_acc

    def column_max(m_acc):
        rows = m_acc.shape[0]
        while rows > 8:
            rows //= 2
            m_acc = jnp.maximum(m_acc[0:rows], m_acc[rows:2 * rows])
        return jnp.max(m_acc, axis=0, keepdims=True)

    def finish(acc):
        return acc[0:V_HEAD] * (1.0 / acc[V_HEAD:V_HEAD + 1])

    out_t = []
    m_prev = None
    for hd in range(MLA_HEADS + 1):
        m_acc = None
        acc = jnp.zeros((VT_ROWS, tq), F32)
        for lo, size in chunks:
            if hd < MLA_HEADS:
                m_acc = scores(hd, lo, size, m_acc)
            if hd > 0:
                p_t = jnp.exp2(s_scr[(hd - 1) % 2, lo:lo + size, :] - m_prev).astype(BF16)
                acc = acc + jnp.dot(vt_ref[0, (hd - 1) * VT_ROWS:hd * VT_ROWS, lo:lo + size], p_t,
                                    preferred_element_type=F32)
        if hd > 0:
            out_t.append(finish(acc))
        if hd < MLA_HEADS:
            m_prev = column_max(m_acc)
    for pair in range(MLA_HEADS // 2):
        o_ref[0, :, pair * HEAD_SLAB:(pair + 1) * HEAD_SLAB] = (
            jnp.concatenate(out_t[2 * pair:2 * pair + 2], axis=0).T.astype(BF16))


def _attention(q, k, vt, seq, ctx, prev_out=None):
    bsz, t, _ = q.shape
    latent = prev_out is None
    tq = ATT_TQ_LATENT if latent else ATT_TQ
    n_tiles = (seq if latent else ctx) // tq
    q_off = 0 if latent else seq // tq
    n_keys = t if latent else ctx
    key_blk = 0 if latent else seq // ctx
    nh = MLA_HEADS * HEAD_SLAB
    in_specs = [pl.BlockSpec((1, tq, nh), lambda b, i: (b, q_off + i, 0)),
                pl.BlockSpec((1, n_keys, nh), lambda b, i: (b, key_blk, 0)),
                pl.BlockSpec((1, MLA_HEADS * VT_ROWS, n_keys), lambda b, i: (b, 0, key_blk))]
    args = [q, k, vt]
    if not latent:
        in_specs.append(pl.BlockSpec(memory_space=pl.ANY))
        args.append(prev_out)
    return pl.pallas_call(
        _attn_kernel,
        grid=(bsz, n_tiles),
        in_specs=in_specs,
        out_specs=pl.BlockSpec((1, tq, MLA_OUT), lambda b, i: (b, q_off + i, 0)),
        out_shape=jax.ShapeDtypeStruct((bsz, t, MLA_OUT), BF16),
        scratch_shapes=[pltpu.VMEM((2, n_keys, tq), F32)],
        input_output_aliases={} if latent else {3: 0},
        compiler_params=pltpu.CompilerParams(dimension_semantics=("parallel", "arbitrary"),
                                             vmem_limit_bytes=VMEM_LIMIT),
        name="mla_attention" if latent else "mla_attention_ctx",
    )(*args)


def _pool_mix(pool_ref, pprev_ref, pnext_ref, ext_scr, pw_ref, ps_ref, tile, n_lat_tiles, n_tiles, seq, ctx):
    seg_first = jnp.logical_or(tile == 0, tile == n_lat_tiles)
    seg_last = jnp.logical_or(tile == n_lat_tiles - 1, tile == n_tiles - 1)
    h0 = POOL_HALO
    ext_scr[0:h0, :] = pprev_ref[0] * jnp.where(seg_first, 0.0, 1.0)
    ext_scr[h0:h0 + TM, :] = pool_ref[0]
    ext_scr[h0 + TM:h0 + TM + h0, :] = pnext_ref[0] * jnp.where(seg_last, 0.0, 1.0)

    is_ctx = tile >= n_lat_tiles
    seg_len = jnp.where(is_ctx, ctx, seq)
    pos = lax.broadcasted_iota(jnp.int32, (TM, 1), 0) + jnp.where(is_ctx, tile - n_lat_tiles, tile) * TM
    lane = lax.broadcasted_iota(jnp.int32, (1, 128), 1)
    low_half = lane < POOL_GROUP_DIM

    def inv_count(w):
        cnt = jnp.minimum(pos + (w - w // 2), seg_len) - jnp.maximum(pos - w // 2, 0)
        return 1.0 / cnt.astype(F32)

    def taps(slab, offsets):
        tot = None
        for k in offsets:
            piece = ext_scr[h0 + k:h0 + k + TM, slab * 128:(slab + 1) * 128]
            tot = piece if tot is None else tot + piece
        return tot

    outs = []
    for slab in range(2):
        w_small, w_big = POOL_WINDOWS[2 * slab], POOL_WINDOWS[2 * slab + 1]
        small = taps(slab, range(-(w_small // 2), w_small - w_small // 2))
        extra = [k for k in range(-(w_big // 2), w_big - w_big // 2)
                 if not -(w_small // 2) <= k < w_small - w_small // 2]
        big = small + taps(slab, extra)
        mean = jnp.where(low_half, small * inv_count(w_small), big * inv_count(w_big))
        outs.append(mean - ext_scr[h0:h0 + TM, slab * 128:(slab + 1) * 128])
    d = jnp.concatenate(outs, axis=1).astype(BF16)
    return jnp.dot(d, pw_ref[...], preferred_element_type=F32) * ps_ref[...]


def _mix_mlp_kernel(*refs, final, n_lat_tiles, n_tiles, seq, ctx, split_ctx):
    x_ref, ctx_ref = (refs[0], refs[1]) if split_ctx else (refs[0], None)
    refs = refs[2 if split_ctx else 1:]
    if final:
        (ssd_ref, attn_ref, pool_ref, pprev_ref, pnext_ref, mod_ref, wos_ref, woa_ref, wop_ref,
         pw_ref, ps_ref, n2_ref, w1_ref, w2_ref, fn_ref, o_ref, ext_scr) = refs
    else:
        (ssd_ref, attn_ref, pool_ref, pprev_ref, pnext_ref, mod_ref, wos_ref, woa_ref, wop_ref,
         pw_ref, ps_ref, n2_ref, w1_ref, w2_ref, o_ref, ext_scr) = refs
    tile = pl.program_id(1)
    pool_y = _pool_mix(pool_ref, pprev_ref, pnext_ref, ext_scr, pw_ref, ps_ref, tile,
                       n_lat_tiles, n_tiles, seq, ctx)
    mix = (jnp.dot(ssd_ref[0], wos_ref[...], preferred_element_type=F32)
           + jnp.dot(attn_ref[0], woa_ref[...], preferred_element_type=F32)
           + jnp.dot(pool_y.astype(BF16), wop_ref[...], preferred_element_type=F32))
    x1 = _token_tile(x_ref, ctx_ref, tile, n_lat_tiles) + mod_ref[0, 2:3, :] * mix
    h = _rms(x1, n2_ref[...])
    hb = (h * (1.0 + mod_ref[0, 4:5, :]) + mod_ref[0, 3:4, :]).astype(BF16)
    acc = jnp.zeros((TM, D_MODEL), F32)
    for lo in range(0, D_FF, FF_CHUNK):
        a = jnp.maximum(jnp.dot(hb, w1_ref[:, lo:lo + FF_CHUNK], preferred_element_type=F32), 0.0)
        acc = acc + jnp.dot((a * a).astype(BF16), w2_ref[lo:lo + FF_CHUNK, :], preferred_element_type=F32)
    x2 = x1 + mod_ref[0, 5:6, :] * acc
    if final:
        x2 = _rms(x2, fn_ref[...])
    o_ref[0] = x2


def _mix_mlp(xc, ctx_in, ssd, attn, pool, mod, wos, woa, wop, pw, ps, n2, w1, w2, fn, n_lat_tiles, seq, ctx,
             final):
    bsz, t, _ = ssd.shape
    split_ctx = ctx_in is not None
    n_tiles = t // TM
    out_tiles = n_lat_tiles if final else n_tiles
    halo_blocks = t // POOL_HALO
    per = TM // POOL_HALO
    ctx_row = mod.shape[0] - 1
    last_x_tile = xc.shape[1] // TM - 1

    def tok(width):
        return pl.BlockSpec((1, TM, width), lambda b, i: (b, i, 0))

    x_specs = [pl.BlockSpec((1, TM, D_MODEL), lambda b, i: (b, jnp.minimum(i, last_x_tile), 0))]
    x_args = [xc]
    if split_ctx:
        x_specs.append(pl.BlockSpec((1, TM, D_MODEL), lambda b, i: (b, 0, 0)))
        x_args.append(ctx_in)

    def const(shape):
        return pl.BlockSpec(shape, lambda b, i: (0,) * len(shape), pipeline_mode=pl.Buffered(1))

    prev_spec = pl.BlockSpec((1, POOL_HALO, POOL_DIM), lambda b, i: (b, jnp.maximum(i * per - 1, 0), 0))
    next_spec = pl.BlockSpec((1, POOL_HALO, POOL_DIM),
                             lambda b, i: (b, jnp.minimum((i + 1) * per, halo_blocks - 1), 0))
    mod_spec = pl.BlockSpec((1, 6, D_MODEL), lambda b, i: (jnp.where(i >= n_lat_tiles, ctx_row, b), 0, 0))
    weights = [wos, woa, wop, pw, ps, n2, w1, w2] + ([fn] if final else [])
    return pl.pallas_call(
        functools.partial(_mix_mlp_kernel, final=final, n_lat_tiles=n_lat_tiles, n_tiles=n_tiles,
                          seq=seq, ctx=ctx, split_ctx=split_ctx),
        grid=(bsz, out_tiles),
        in_specs=x_specs + [tok(SSD_PAD), tok(MLA_OUT), tok(POOL_DIM), prev_spec, next_spec, mod_spec]
                 + [const(w.shape) for w in weights],
        out_specs=tok(D_MODEL),
        out_shape=jax.ShapeDtypeStruct((bsz, out_tiles * TM, D_MODEL), F32),
        scratch_shapes=[pltpu.VMEM((TM + 2 * POOL_HALO, POOL_DIM), F32)],
        compiler_params=pltpu.CompilerParams(dimension_semantics=("parallel", "parallel"),
                                             vmem_limit_bytes=VMEM_LIMIT),
        name="mix_mlp_final" if final else "mix_mlp",
    )(*x_args, ssd, attn, pool, pool, pool, mod, *weights)


def kernel(x, c, ctx, c_ctx, mod_w, mod_b, norm1_w, norm2_w, w_in, conv_w, conv_b, dt_bias, a_log, ssd_d,
           ssd_norm_w, q_a_norm_w, w_q_b, kv_a_norm_w, w_kv_b, pool_w, pool_scale, w_out, w_mlp1, w_mlp2,
           final_norm_w):
    bsz, seq, _ = x.shape
    n_ctx = ctx.shape[1]
    depth = mod_w.shape[0]
    assert seq % TM == 0 and n_ctx % TM == 0 and seq % GRID_W == 0
    assert seq % SSD_CHUNK == 0 and n_ctx % SSD_CHUNK == 0
    assert seq % ATT_TQ_LATENT == 0 and n_ctx % ATT_TQ == 0 and seq % n_ctx == 0
    n_lat_tiles = seq // TM
    n_lat_chunks = seq // SSD_CHUNK

    assert n_ctx == TM
    xc, ctx_in = x, ctx
    cond_rows = -(-(bsz + 1) // 8) * 8
    cond = jnp.zeros((cond_rows, D_MODEL), F32).at[:bsz].set(c).at[bsz].set(c_ctx)
    mod_all = _modulation(cond, mod_w, mod_b)[:, :bsz + 1]
    rope = _rope_tables(seq, n_ctx)

    win_idx, wq_idx, wkv_idx = _win_index(), _wq_index(), _wkv_index()
    pad_idx = _pad_group_index(0)
    xbc_idx = np.concatenate([pad_idx, SSD_INNER + np.arange(XBC_DIM - SSD_INNER)])
    row = lambda v: v.reshape(1, -1)

    for i in range(depth):
        final = i == depth - 1
        win = _remap_cols(w_in[i], win_idx).astype(BF16)
        wq = _remap_cols(w_q_b[i], wq_idx).astype(BF16)
        wkv = _remap_cols(w_kv_b[i], wkv_idx).astype(BF16)
        convw = _remap_cols(conv_w[i], xbc_idx)
        convb = row(_remap_cols(conv_b[i], xbc_idx))
        dtb = row(jnp.pad(dt_bias[i].reshape(-1), (0, 128 - 2 * SSD_HEADS)))
        alog = row(jnp.pad(a_log[i].reshape(-1), (0, 128 - 2 * SSD_HEADS)))
        dskip = row(_remap_cols(jnp.repeat(ssd_d[i], SSD_HEAD_DIM), pad_idx))
        nw = row(_remap_cols(ssd_norm_w[i], pad_idx))
        wo = w_out[i].astype(BF16)
        wos = _remap_cols(wo[:SSD_INNER].T, pad_idx).T
        woa = wo[SSD_INNER:SSD_INNER + MLA_OUT]
        wop = wo[SSD_INNER + MLA_OUT:]
        pw = jax.scipy.linalg.block_diag(*[pool_w[i, g] for g in range(len(POOL_WINDOWS))]).astype(BF16)

        z, xs, bc, dt, pool, q, k, v = _inproj(xc, ctx_in, mod_all[i], row(norm1_w[i]), win, convw, convb,
                                                row(q_a_norm_w[i]), row(kv_a_norm_w[i]), wq, wkv, rope,
                                                n_lat_tiles)
        yb = _ssd_scan(xs, bc, dt, None, None, dtb, alog, None, None, n_lat_chunks, rev=True)
        ssd = _ssd_scan(xs, bc, dt, z, yb, dtb, alog, dskip, nw, n_lat_chunks, rev=False)
        attn = _attention(q, k, v, seq, n_ctx)
        if not final:
            attn = _attention(q, k, v, seq, n_ctx, prev_out=attn)
        xc = _mix_mlp(xc, ctx_in, ssd, attn, pool, mod_all[i], wos, woa, wop, pw, row(pool_scale[i]),
                      row(norm2_w[i]), w_mlp1[i].astype(BF16), w_mlp2[i].astype(BF16), row(final_norm_w),
                      n_lat_tiles, seq, n_ctx, final)
        ctx_in = None
    return xc
```

```python
import functools
import math

import jax
import jax.numpy as jnp
import numpy as np
from jax import lax
from jax.experimental import pallas as pl
from jax.experimental.pallas import tpu as pltpu

F32 = jnp.float32
BF16 = jnp.bfloat16

D_MODEL = 1024
GRID_W = 64
EPS = 1e-6

SSD_HEADS = 6
SSD_HEAD_DIM = 64
SSD_GROUPS = 2
SSD_STATE = 128
SSD_CONV = 4
SSD_CHUNK = 128
SSD_STEP_CHUNKS = 2
HEADS_PER_GROUP = SSD_HEADS // SSD_GROUPS
GROUP_DIM = HEADS_PER_GROUP * SSD_HEAD_DIM
GROUP_PAD = 256
SSD_INNER = SSD_HEADS * SSD_HEAD_DIM
SSD_PAD = SSD_GROUPS * GROUP_PAD
XBC_DIM = SSD_INNER + 2 * SSD_GROUPS * SSD_STATE
XBC_PAD = SSD_PAD + 2 * SSD_GROUPS * SSD_STATE

MLA_HEADS = 6
Q_LORA = 256
KV_LORA = 256
QK_NOPE = 64
QK_ROPE = 32
V_HEAD = 64
QK_DIM = QK_NOPE + QK_ROPE
MLA_OUT = MLA_HEADS * V_HEAD
ROPE_THETA = 10000.0
ROPE_PAIRS = QK_ROPE // 4
HEAD_SLAB = 128

POOL_WINDOWS = (2, 4, 8, 16)
POOL_GROUP_DIM = 64
POOL_DIM = len(POOL_WINDOWS) * POOL_GROUP_DIM
POOL_HALO = 8
CONV_HALO = 8

D_FF = 4 * D_MODEL
FF_CHUNK = 1024

OFF_Z = 0
OFF_XBC = OFF_Z + SSD_INNER
OFF_DT = OFF_XBC + XBC_DIM
OFF_QA = OFF_DT + 2 * SSD_HEADS
OFF_KVA = OFF_QA + Q_LORA
OFF_KROPE = OFF_KVA + KV_LORA
OFF_POOL = OFF_KROPE + QK_ROPE
IN_COLS = OFF_POOL + POOL_DIM

P_Z = 0
P_XBC = P_Z + SSD_PAD
P_QA = P_XBC + XBC_PAD
P_KVA = P_QA + Q_LORA
P_POOL = P_KVA + KV_LORA
P_DT = P_POOL + POOL_DIM
P_KR = P_DT + 128
P_COLS = P_KR + HEAD_SLAB

TM = 256
ATT_TQ = 256
ATT_TQ_LATENT = 512
ATT_TK = 512
VT_ROWS = 80
NEG_BIG = -1e30
VMEM_LIMIT = 56 * 1024 * 1024
LOG2E = math.log2(math.e)


def _pad_group_index(base):
    idx = np.full((SSD_PAD,), -1, np.int64)
    for g in range(SSD_GROUPS):
        idx[g * GROUP_PAD:g * GROUP_PAD + GROUP_DIM] = base + g * GROUP_DIM + np.arange(GROUP_DIM)
    return idx


def _rope_perm():
    src = np.zeros((QK_ROPE,), np.int64)
    for half in range(2):
        for axis in range(2):
            for p in range(ROPE_PAIRS):
                src[half * 16 + axis * 8 + p] = axis * 16 + half * 8 + p
    return src


def _win_index():
    idx = np.full((P_COLS,), -1, np.int64)
    idx[P_Z:P_Z + SSD_PAD] = _pad_group_index(OFF_Z)
    idx[P_XBC:P_XBC + SSD_PAD] = _pad_group_index(OFF_XBC)
    nbc = 2 * SSD_GROUPS * SSD_STATE
    idx[P_XBC + SSD_PAD:P_XBC + SSD_PAD + nbc] = OFF_XBC + SSD_INNER + np.arange(nbc)
    idx[P_QA:P_QA + Q_LORA] = OFF_QA + np.arange(Q_LORA)
    idx[P_KVA:P_KVA + KV_LORA] = OFF_KVA + np.arange(KV_LORA)
    idx[P_POOL:P_POOL + POOL_DIM] = OFF_POOL + np.arange(POOL_DIM)
    idx[P_DT:P_DT + 2 * SSD_HEADS] = OFF_DT + np.arange(2 * SSD_HEADS)
    idx[P_KR + QK_NOPE:P_KR + QK_NOPE + QK_ROPE] = OFF_KROPE + _rope_perm()
    return idx


def _wq_index():
    idx = np.full((MLA_HEADS * HEAD_SLAB,), -1, np.int64)
    perm = _rope_perm()
    for h in range(MLA_HEADS):
        idx[h * HEAD_SLAB:h * HEAD_SLAB + QK_NOPE] = h * QK_DIM + np.arange(QK_NOPE)
        idx[h * HEAD_SLAB + QK_NOPE:h * HEAD_SLAB + QK_DIM] = h * QK_DIM + QK_NOPE + perm
    return idx


def _wkv_index():
    nk = MLA_HEADS * HEAD_SLAB
    idx = np.full((nk + MLA_OUT,), -1, np.int64)
    per_head = QK_NOPE + V_HEAD
    for h in range(MLA_HEADS):
        idx[h * HEAD_SLAB:h * HEAD_SLAB + QK_NOPE] = h * per_head + np.arange(QK_NOPE)
        idx[nk + h * V_HEAD:nk + (h + 1) * V_HEAD] = h * per_head + QK_NOPE + np.arange(V_HEAD)
    return idx


def _remap_cols(w, idx):
    take = jnp.take(w, jnp.asarray(np.maximum(idx, 0)), axis=-1)
    return jnp.where(jnp.asarray(idx >= 0), take, jnp.zeros((), w.dtype))


def _rope_tables(seq, ctx):
    t = np.arange(seq)
    row = jnp.asarray((t // GRID_W).astype(np.float32))
    col = jnp.asarray((t % GRID_W).astype(np.float32))
    inv_freq = ROPE_THETA ** (-jnp.arange(ROPE_PAIRS, dtype=F32) / ROPE_PAIRS)
    ang = jnp.concatenate([row[:, None] * inv_freq, col[:, None] * inv_freq], axis=1)
    cos16, sin16 = jnp.cos(ang), jnp.sin(ang)
    ones = jnp.ones((seq, QK_NOPE), F32)
    zeros = jnp.zeros((seq, QK_NOPE), F32)
    z16 = jnp.zeros((seq, 16), F32)
    tail1 = jnp.ones((seq, HEAD_SLAB - QK_DIM), F32)
    tail0 = jnp.zeros((seq, HEAD_SLAB - QK_DIM), F32)
    c_tab = jnp.concatenate([ones, cos16, cos16, tail1], axis=1)
    s1_tab = jnp.concatenate([zeros, -sin16, z16, tail0], axis=1)
    s2_tab = jnp.concatenate([zeros, z16, sin16, tail0], axis=1)
    c_tab = jnp.concatenate([c_tab, jnp.ones((ctx, HEAD_SLAB), F32)], axis=0)
    s1_tab = jnp.concatenate([s1_tab, jnp.zeros((ctx, HEAD_SLAB), F32)], axis=0)
    s2_tab = jnp.concatenate([s2_tab, jnp.zeros((ctx, HEAD_SLAB), F32)], axis=0)
    k_tabs = jnp.stack([c_tab, s1_tab, s2_tab])
    qscale = (QK_DIM ** -0.5) * LOG2E
    return jnp.concatenate([k_tabs, k_tabs * qscale], axis=0)


def _rms(u, w_row):
    ms = jnp.mean(u * u, axis=-1, keepdims=True)
    return u * lax.rsqrt(ms + EPS) * w_row


def _silu(u):
    return u * (1.0 / (1.0 + jnp.exp(-u)))


def _softplus(u):
    return jnp.maximum(u, 0.0) + jnp.log1p(jnp.exp(-jnp.abs(u)))


def _mod_kernel(cond_ref, w_ref, b_ref, o_ref):
    cnd = cond_ref[...]
    o_ref[0] = jnp.dot(_silu(cnd), w_ref[0], preferred_element_type=F32,
                       precision=lax.Precision.HIGHEST) + b_ref[0]


def _modulation(cond, mod_w, mod_b):
    depth = mod_w.shape[0]
    r = cond.shape[0]
    out = pl.pallas_call(
        _mod_kernel,
        grid=(depth, 6),
        in_specs=[pl.BlockSpec((r, D_MODEL), lambda d, j: (0, 0)),
                  pl.BlockSpec((1, D_MODEL, D_MODEL), lambda d, j: (d, 0, j)),
                  pl.BlockSpec((1, 1, D_MODEL), lambda d, j: (d, 0, j))],
        out_specs=pl.BlockSpec((1, r, D_MODEL), lambda d, j: (d, 0, j)),
        out_shape=jax.ShapeDtypeStruct((depth, r, 6 * D_MODEL), F32),
        name="adaln_modulation",
    )(cond, mod_w, mod_b.reshape(depth, 1, 6 * D_MODEL))
    return out.reshape(depth, r, 6, D_MODEL)


def _rope(t, c, s1, s2):
    return t * c + pltpu.roll(t, HEAD_SLAB - 16, 1) * s1 + pltpu.roll(t, 16, 1) * s2


def _token_tile(x_ref, ctx_ref, tile, n_lat_tiles):
    if ctx_ref is None:
        return x_ref[0]
    return jnp.where(tile >= n_lat_tiles, ctx_ref[0], x_ref[0])


def _inproj_kernel(*refs, n_lat_tiles, n_tiles, split_ctx):
    x_ref, ctx_ref = (refs[0], refs[1]) if split_ctx else (refs[0], None)
    (xprev_ref, xnext_ref, mod_ref, n1_ref, win_ref, convw_ref, convb_ref, qan_ref, kvan_ref, wq_ref, wkv_ref,
     rope_ref, z_ref, xs_ref, bc_ref, dt_ref, pool_ref, q_ref, k_ref, vt_ref, ext_scr) = refs[2 if split_ctx else 1:]

    def norm_mod(x):
        return (_rms(x, n1_ref[...]) * (1.0 + mod_ref[0, 1:2, :]) + mod_ref[0, 0:1, :]).astype(BF16)

    hb = norm_mod(_token_tile(x_ref, ctx_ref, pl.program_id(1), n_lat_tiles))

    def proj(lo, width):
        return jnp.dot(hb, win_ref[:, lo:lo + width], preferred_element_type=F32)

    z_ref[0] = proj(P_Z, SSD_PAD)
    pool_ref[0] = proj(P_POOL, POOL_DIM)

    tile = pl.program_id(1)
    seg_first = jnp.logical_or(tile == 0, tile == n_lat_tiles)
    seg_last = jnp.logical_or(tile == n_lat_tiles - 1, tile == n_tiles - 1)
    halo = norm_mod(jnp.concatenate([xprev_ref[0], xnext_ref[0]], axis=0))
    xbc = jnp.dot(jnp.concatenate([hb, halo], axis=0), win_ref[:, P_XBC:P_XBC + XBC_PAD],
                  preferred_element_type=F32)
    ext_scr[0:CONV_HALO, :] = xbc[TM:TM + CONV_HALO] * jnp.where(seg_first, 0.0, 1.0)
    ext_scr[CONV_HALO:CONV_HALO + TM, :] = xbc[0:TM]
    ext_scr[CONV_HALO + TM:, :] = xbc[TM + CONV_HALO:] * jnp.where(seg_last, 0.0, 1.0)
    u = convb_ref[...]
    for tap in range(SSD_CONV):
        lo = CONV_HALO - 1 + tap
        u = u + ext_scr[lo:lo + TM, :] * convw_ref[tap:tap + 1, :]
    u = _silu(u)
    xs_ref[0] = u[:, :SSD_PAD]
    bc_ref[0] = u[:, SSD_PAD:].astype(BF16)
    dt_ref[0] = proj(P_DT, 128)

    cq = _rms(proj(P_QA, Q_LORA), qan_ref[...]).astype(BF16)
    ckv = _rms(proj(P_KVA, KV_LORA), kvan_ref[...]).astype(BF16)
    q = jnp.dot(cq, wq_ref[...], preferred_element_type=F32)
    kv = jnp.dot(ckv, wkv_ref[...], preferred_element_type=F32)
    k_rope = _rope(proj(P_KR, HEAD_SLAB), rope_ref[0], rope_ref[1], rope_ref[2])
    for hd in range(MLA_HEADS):
        sl = slice(hd * HEAD_SLAB, (hd + 1) * HEAD_SLAB)
        q_ref[0, :, sl] = _rope(q[:, sl], rope_ref[3], rope_ref[4], rope_ref[5]).astype(BF16)
        k_ref[0, :, sl] = (kv[:, sl] + k_rope).astype(BF16)
    nk = MLA_HEADS * HEAD_SLAB
    extra = VT_ROWS - V_HEAD
    ones_rows = (lax.broadcasted_iota(jnp.int32, (extra, TM), 0) == 0).astype(BF16)
    for pair in range(MLA_HEADS // 2):
        vt = kv[:, nk + pair * HEAD_SLAB:nk + (pair + 1) * HEAD_SLAB].T.astype(BF16)
        for sub in range(2):
            base = (2 * pair + sub) * VT_ROWS
            vt_ref[0, base:base + V_HEAD, :] = vt[sub * V_HEAD:(sub + 1) * V_HEAD, :]
            vt_ref[0, base + V_HEAD:base + VT_ROWS, :] = ones_rows


def _inproj(xc, ctx_in, mod, n1, win, convw, convb, qan, kvan, wq, wkv, rope, n_lat_tiles):
    bsz = xc.shape[0]
    split_ctx = ctx_in is not None
    t = xc.shape[1] + (ctx_in.shape[1] if split_ctx else 0)
    nt = t // TM
    ctx_row = mod.shape[0] - 1
    halo_blocks = xc.shape[1] // CONV_HALO
    last_x_tile = xc.shape[1] // TM - 1
    per = TM // CONV_HALO

    def tok(width):
        return pl.BlockSpec((1, TM, width), lambda b, i: (b, i, 0))

    x_specs = [pl.BlockSpec((1, TM, D_MODEL), lambda b, i: (b, jnp.minimum(i, last_x_tile), 0))]
    x_args = [xc]
    if split_ctx:
        x_specs.append(pl.BlockSpec((1, TM, D_MODEL), lambda b, i: (b, 0, 0)))
        x_args.append(ctx_in)

    def const(shape):
        return pl.BlockSpec(shape, lambda b, i: (0,) * len(shape))

    prev_spec = pl.BlockSpec((1, CONV_HALO, D_MODEL), lambda b, i: (b, jnp.maximum(i * per - 1, 0), 0))
    next_spec = pl.BlockSpec((1, CONV_HALO, D_MODEL),
                             lambda b, i: (b, jnp.minimum((i + 1) * per, halo_blocks - 1), 0))
    mod_spec = pl.BlockSpec((1, 6, D_MODEL), lambda b, i: (jnp.where(i >= n_lat_tiles, ctx_row, b), 0, 0))
    nh = MLA_HEADS * HEAD_SLAB
    nbc = 2 * SSD_GROUPS * SSD_STATE
    outs = [(SSD_PAD, F32), (SSD_PAD, F32), (nbc, BF16), (128, F32), (POOL_DIM, F32), (nh, BF16), (nh, BF16)]
    vt_rows = MLA_HEADS * VT_ROWS
    return pl.pallas_call(
        functools.partial(_inproj_kernel, n_lat_tiles=n_lat_tiles, n_tiles=nt, split_ctx=split_ctx),
        grid=(bsz, nt),
        in_specs=x_specs + [prev_spec, next_spec, mod_spec, const((1, D_MODEL)), const((D_MODEL, P_COLS)),
                  const(convw.shape), const(convb.shape),
                  const((1, Q_LORA)), const((1, KV_LORA)), const((Q_LORA, nh)), const((KV_LORA, nh + MLA_OUT)),
                  pl.BlockSpec((6, TM, HEAD_SLAB), lambda b, i: (0, i, 0))],
        out_specs=[tok(w) for w, _ in outs] + [pl.BlockSpec((1, vt_rows, TM), lambda b, i: (b, 0, i))],
        out_shape=[jax.ShapeDtypeStruct((bsz, t, w), dt) for w, dt in outs]
                  + [jax.ShapeDtypeStruct((bsz, vt_rows, t), BF16)],
        scratch_shapes=[pltpu.VMEM((TM + 2 * CONV_HALO, XBC_PAD), F32)],
        compiler_params=pltpu.CompilerParams(dimension_semantics=("parallel", "parallel"),
                                             vmem_limit_bytes=VMEM_LIMIT),
        name="inproj_mla",
    )(*x_args, xc, xc, mod, n1, win, convw, convb, qan, kvan, wq, wkv, rope)


def _expand_heads(cols, first_lane):
    n = cols.shape[0]
    lane = lax.broadcasted_iota(jnp.int32, (n, 128), 1)
    parts = []
    for g in range(SSD_GROUPS):
        b = [jnp.broadcast_to(cols[:, first_lane + HEADS_PER_GROUP * g + i:first_lane + HEADS_PER_GROUP * g + i + 1],
                              (n, 128)) for i in range(HEADS_PER_GROUP)]
        parts.append(jnp.where(lane < SSD_HEAD_DIM, b[0], b[1]))
        parts.append(b[2])
    return jnp.concatenate(parts, axis=1)


def _ssd_kernel(*refs, rev):
    h_scr = refs[-1]

    @pl.when(pl.program_id(1) == 0)
    def _():
        h_scr[...] = jnp.zeros_like(h_scr)

    order = range(SSD_STEP_CHUNKS - 1, -1, -1) if rev else range(SSD_STEP_CHUNKS)
    for sub in order:
        _ssd_one_chunk(refs, rev, slice(sub * SSD_CHUNK, (sub + 1) * SSD_CHUNK))


def _ssd_one_chunk(refs, rev, rows):
    if rev:
        xs_ref, bc_ref, dt_ref, dtb_ref, alog_ref, out_ref, h_scr = refs
    else:
        xs_ref, bc_ref, dt_ref, z_ref, yb_ref, dtb_ref, alog_ref, dskip_ref, nw_ref, out_ref, h_scr = refs
    L = SSD_CHUNK
    xs = xs_ref[0, rows, :]
    bmat = bc_ref[0, rows, 0:SSD_GROUPS * SSD_STATE]
    cmat = bc_ref[0, rows, SSD_GROUPS * SSD_STATE:]

    first_lane = SSD_HEADS if rev else 0
    dtv = _softplus(dt_ref[0, rows, :] + dtb_ref[...])
    adt = dtv * (-jnp.exp(alog_ref[...]))
    ri = lax.broadcasted_iota(jnp.int32, (L, L), 0)
    ci = lax.broadcasted_iota(jnp.int32, (L, L), 1)
    causal = (ci >= ri) if rev else (ci <= ri)
    cs_col = jnp.dot(causal.astype(F32), adt, preferred_element_type=F32,
                     precision=lax.Precision.HIGHEST)
    cs_row = cs_col.T

    dt_e = _expand_heads(dtv, first_lane)
    cs_e = _expand_heads(cs_col, first_lane)
    end_row = 0 if rev else L - 1
    cs_end = cs_e[end_row:end_row + 1, :]
    xdt = xs * dt_e
    xdt_b = xdt.astype(BF16)
    xdd_b = (xdt * jnp.exp(cs_end - cs_e)).astype(BF16)
    in_decay = jnp.exp(cs_e)
    state_decay = jnp.exp(cs_end)

    lane_g = lax.broadcasted_iota(jnp.int32, (1, GROUP_PAD), 1)
    ys = []
    for g in range(SSD_GROUPS):
        gs = slice(g * GROUP_PAD, (g + 1) * GROUP_PAD)
        ns = slice(g * SSD_STATE, (g + 1) * SSD_STATE)
        cm_g, bm_g = cmat[:, ns], bmat[:, ns]
        cb = lax.dot_general(cm_g, bm_g, (((1,), (1,)), ((), ())), preferred_element_type=F32)
        x_g = xdt_b[:, gs]
        y_g = jnp.zeros((L, GROUP_PAD), F32)
        for i in range(HEADS_PER_GROUP):
            hl = first_lane + HEADS_PER_GROUP * g + i
            diff = cs_col[:, hl:hl + 1] - cs_row[hl:hl + 1, :]
            decay = jnp.exp(jnp.where(causal, diff, NEG_BIG))
            head_lanes = jnp.logical_and(lane_g >= i * SSD_HEAD_DIM, lane_g < (i + 1) * SSD_HEAD_DIM)
            x_h = jnp.where(head_lanes, x_g, jnp.zeros((), BF16))
            y_g = y_g + jnp.dot((cb * decay).astype(BF16), x_h, preferred_element_type=F32)
        h_t = h_scr[g]
        y_in = jnp.dot(cm_g, h_t.astype(BF16), preferred_element_type=F32)
        y_g = y_g + y_in * in_decay[:, gs]
        s_new = lax.dot_general(bm_g, xdd_b[:, gs], (((0,), (0,)), ((), ())), preferred_element_type=F32)
        h_scr[g] = h_t * state_decay[:, gs] + s_new
        ys.append(y_g)
    y = jnp.concatenate(ys, axis=1)

    if rev:
        out_ref[0, rows, :] = y
    else:
        y = y + yb_ref[0, rows, :] + xs * dskip_ref[...]
        gated = y * _silu(z_ref[0, rows, :])
        outs = []
        for g in range(SSD_GROUPS):
            sl = gated[:, g * GROUP_PAD:(g + 1) * GROUP_PAD]
            ms = jnp.sum(sl * sl, axis=-1, keepdims=True) * (1.0 / GROUP_DIM)
            outs.append(sl * lax.rsqrt(ms + EPS))
        out_ref[0, rows, :] = (jnp.concatenate(outs, axis=1) * nw_ref[...]).astype(BF16)


def _ssd_tile(j, rev, n_lat_tiles, n_tiles):
    if rev:
        return n_tiles - 1 - j
    n_ctx = n_tiles - n_lat_tiles
    return jnp.where(j < n_ctx, n_lat_tiles + j, j - n_ctx)


def _ssd_scan(xs, bc, dt, z, yb, dtb, alog, dskip, nw, n_lat_chunks, rev):
    bsz, t, _ = xs.shape
    step_rows = SSD_STEP_CHUNKS * SSD_CHUNK
    n_chunks = t // step_rows
    chunk = functools.partial(_ssd_tile, rev=rev, n_lat_tiles=n_lat_chunks // SSD_STEP_CHUNKS, n_tiles=n_chunks)

    def tok(width):
        return pl.BlockSpec((1, step_rows, width), lambda b, j: (b, chunk(j), 0))

    def const(shape):
        return pl.BlockSpec(shape, lambda b, j: (0,) * len(shape))

    params = [dtb, alog]
    param_specs = [const(p.shape) for p in params]
    if rev:
        args = [xs, bc, dt] + params
        in_specs = [tok(SSD_PAD), tok(bc.shape[-1]), tok(128)] + param_specs
        out_dtype = F32
    else:
        args = [xs, bc, dt, z, yb] + params + [dskip, nw]
        in_specs = ([tok(SSD_PAD), tok(bc.shape[-1]), tok(128), tok(SSD_PAD), tok(SSD_PAD)] + param_specs
                    + [const(dskip.shape), const(nw.shape)])
        out_dtype = BF16
    return pl.pallas_call(
        functools.partial(_ssd_kernel, rev=rev),
        grid=(bsz, n_chunks),
        in_specs=in_specs,
        out_specs=tok(SSD_PAD),
        out_shape=jax.ShapeDtypeStruct((bsz, t, SSD_PAD), out_dtype),
        scratch_shapes=[pltpu.VMEM((SSD_GROUPS, SSD_STATE, GROUP_PAD), F32)],
        compiler_params=pltpu.CompilerParams(dimension_semantics=("parallel", "arbitrary"),
                                             vmem_limit_bytes=VMEM_LIMIT),
        name="ssd_bwd" if rev else "ssd_fwd",
    )(*args)


def _attn_kernel(*refs):
    q_ref, k_ref, vt_ref = refs[:3]
    o_ref, s_scr = refs[-2:]
    tq = q_ref.shape[1]
    n_keys = k_ref.shape[1]
    chunks = [(lo, min(ATT_TK, n_keys - lo)) for lo in range(0, n_keys, ATT_TK)]
    mblk = max(8, min(ATT_TK, n_keys, (8 * 1024) // tq))

    def scores(hd, lo, size, m_acc):
        sl = slice(hd * HEAD_SLAB, (hd + 1) * HEAD_SLAB)
        s_c = lax.dot_general(k_ref[0, lo:lo + size, sl], q_ref[0, :, sl], (((1,), (1,)), ((), ())),
                              preferred_element_type=F32)
        s_scr[hd % 2, lo:lo + size, :] = s_c
        for r in range(0, size, mblk):
            blk = s_scr[hd % 2, lo + r:lo + r + mblk, :]
            m_acc = blk if m_acc is None else jnp.maximum(m_acc, blk)
        return m_acc

    def column_max(m_acc):
        rows = m_acc.shape[0]
        while rows > 8:
            rows //= 2
            m_acc = jnp.maximum(m_acc[0:rows], m_acc[rows:2 * rows])
        return jnp.max(m_acc, axis=0, keepdims=True)

    def finish(acc):
        return acc[0:V_HEAD] * (1.0 / acc[V_HEAD:V_HEAD + 1])

    out_t = []
    m_prev = None
    for hd in range(MLA_HEADS + 1):
        m_acc = None
        acc = jnp.zeros((VT_ROWS, tq), F32)
        for lo, size in chunks:
            if hd < MLA_HEADS:
                m_acc = scores(hd, lo, size, m_acc)
            if hd > 0:
                p_t = jnp.exp2(s_scr[(hd - 1) % 2, lo:lo + size, :] - m_prev).astype(BF16)
                acc = acc + jnp.dot(vt_ref[0, (hd - 1) * VT_ROWS:hd * VT_ROWS, lo:lo + size], p_t,
                                    preferred_element_type=F32)
        if hd > 0:
            out_t.append(finish(acc))
        if hd < MLA_HEADS:
            m_prev = column_max(m_acc)
    for pair in range(MLA_HEADS // 2):
        o_ref[0, :, pair * HEAD_SLAB:(pair + 1) * HEAD_SLAB] = (
            jnp.concatenate(out_t[2 * pair:2 * pair + 2], axis=0).T.astype(BF16))


def _attention(q, k, vt, seq, ctx, latent):
    bsz, t, _ = q.shape
    tq = ATT_TQ_LATENT if latent else ATT_TQ
    n_tiles = (seq if latent else ctx) // tq
    q_off = 0 if latent else seq // tq
    n_keys = t if latent else ctx
    key_blk = 0 if latent else seq // ctx
    nh = MLA_HEADS * HEAD_SLAB
    in_specs = [pl.BlockSpec((1, tq, nh), lambda b, i: (b, q_off + i, 0)),
                pl.BlockSpec((1, n_keys, nh), lambda b, i: (b, key_blk, 0)),
                pl.BlockSpec((1, MLA_HEADS * VT_ROWS, n_keys), lambda b, i: (b, 0, key_blk))]
    return pl.pallas_call(
        _attn_kernel,
        grid=(bsz, n_tiles),
        in_specs=in_specs,
        out_specs=pl.BlockSpec((1, tq, MLA_OUT), lambda b, i: (b, i, 0)),
        out_shape=jax.ShapeDtypeStruct((bsz, n_tiles * tq, MLA_OUT), BF16),
        scratch_shapes=[pltpu.VMEM((2, n_keys, tq), F32)],
        compiler_params=pltpu.CompilerParams(dimension_semantics=("parallel", "arbitrary"),
                                             vmem_limit_bytes=VMEM_LIMIT),
        name="mla_attention" if latent else "mla_attention_ctx",
    )(q, k, vt)


def _pool_mix(pool_ref, pprev_ref, pnext_ref, ext_scr, pw_ref, ps_ref, tile, n_lat_tiles, n_tiles, seq, ctx):
    seg_first = jnp.logical_or(tile == 0, tile == n_lat_tiles)
    seg_last = jnp.logical_or(tile == n_lat_tiles - 1, tile == n_tiles - 1)
    h0 = POOL_HALO
    ext_scr[0:h0, :] = pprev_ref[0] * jnp.where(seg_first, 0.0, 1.0)
    ext_scr[h0:h0 + TM, :] = pool_ref[0]
    ext_scr[h0 + TM:h0 + TM + h0, :] = pnext_ref[0] * jnp.where(seg_last, 0.0, 1.0)

    is_ctx = tile >= n_lat_tiles
    seg_len = jnp.where(is_ctx, ctx, seq)
    pos = lax.broadcasted_iota(jnp.int32, (TM, 1), 0) + jnp.where(is_ctx, tile - n_lat_tiles, tile) * TM
    lane = lax.broadcasted_iota(jnp.int32, (1, 128), 1)
    low_half = lane < POOL_GROUP_DIM

    def inv_count(w):
        cnt = jnp.minimum(pos + (w - w // 2), seg_len) - jnp.maximum(pos - w // 2, 0)
        return 1.0 / cnt.astype(F32)

    def taps(slab, offsets):
        tot = None
        for k in offsets:
            piece = ext_scr[h0 + k:h0 + k + TM, slab * 128:(slab + 1) * 128]
            tot = piece if tot is None else tot + piece
        return tot

    outs = []
    for slab in range(2):
        w_small, w_big = POOL_WINDOWS[2 * slab], POOL_WINDOWS[2 * slab + 1]
        small = taps(slab, range(-(w_small // 2), w_small - w_small // 2))
        extra = [k for k in range(-(w_big // 2), w_big - w_big // 2)
                 if not -(w_small // 2) <= k < w_small - w_small // 2]
        big = small + taps(slab, extra)
        mean = jnp.where(low_half, small * inv_count(w_small), big * inv_count(w_big))
        outs.append(mean - ext_scr[h0:h0 + TM, slab * 128:(slab + 1) * 128])
    d = jnp.concatenate(outs, axis=1).astype(BF16)
    return jnp.dot(d, pw_ref[...], preferred_element_type=F32) * ps_ref[...]


def _mix_mlp_kernel(*refs, final, n_lat_tiles, n_tiles, seq, ctx, split_ctx, split_attn):
    x_ref, ctx_ref = (refs[0], refs[1]) if split_ctx else (refs[0], None)
    refs = refs[2 if split_ctx else 1:]
    ssd_ref, attn_ref = refs[0], refs[1]
    attn_ctx_ref = refs[2] if split_attn else None
    refs = refs[3 if split_attn else 2:]
    if final:
        (pool_ref, pprev_ref, pnext_ref, mod_ref, wos_ref, woa_ref, wop_ref,
         pw_ref, ps_ref, n2_ref, w1_ref, w2_ref, fn_ref, o_ref, ext_scr) = refs
    else:
        (pool_ref, pprev_ref, pnext_ref, mod_ref, wos_ref, woa_ref, wop_ref,
         pw_ref, ps_ref, n2_ref, w1_ref, w2_ref, o_ref, ext_scr) = refs
    tile = pl.program_id(1)
    pool_y = _pool_mix(pool_ref, pprev_ref, pnext_ref, ext_scr, pw_ref, ps_ref, tile,
                       n_lat_tiles, n_tiles, seq, ctx)
    mix = (jnp.dot(ssd_ref[0], wos_ref[...], preferred_element_type=F32)
           + jnp.dot(_token_tile(attn_ref, attn_ctx_ref, tile, n_lat_tiles), woa_ref[...],
                     preferred_element_type=F32)
           + jnp.dot(pool_y.astype(BF16), wop_ref[...], preferred_element_type=F32))
    x1 = _token_tile(x_ref, ctx_ref, tile, n_lat_tiles) + mod_ref[0, 2:3, :] * mix
    h = _rms(x1, n2_ref[...])
    hb = (h * (1.0 + mod_ref[0, 4:5, :]) + mod_ref[0, 3:4, :]).astype(BF16)
    acc = jnp.zeros((TM, D_MODEL), F32)
    for lo in range(0, D_FF, FF_CHUNK):
        a = jnp.maximum(jnp.dot(hb, w1_ref[:, lo:lo + FF_CHUNK], preferred_element_type=F32), 0.0)
        acc = acc + jnp.dot((a * a).astype(BF16), w2_ref[lo:lo + FF_CHUNK, :], preferred_element_type=F32)
    x2 = x1 + mod_ref[0, 5:6, :] * acc
    if final:
        x2 = _rms(x2, fn_ref[...])
    o_ref[0] = x2


def _mix_mlp(xc, ctx_in, ssd, attn, attn_ctx, pool, mod, wos, woa, wop, pw, ps, n2, w1, w2, fn, n_lat_tiles,
             seq, ctx, final):
    bsz, t, _ = ssd.shape
    split_ctx = ctx_in is not None
    split_attn = attn_ctx is not None
    assert final or split_attn
    n_tiles = t // TM
    out_tiles = n_lat_tiles if final else n_tiles
    halo_blocks = t // POOL_HALO
    per = TM // POOL_HALO
    ctx_row = mod.shape[0] - 1
    last_x_tile = xc.shape[1] // TM - 1

    def tok(width):
        return pl.BlockSpec((1, TM, width), lambda b, i: (b, i, 0))

    x_specs = [pl.BlockSpec((1, TM, D_MODEL), lambda b, i: (b, jnp.minimum(i, last_x_tile), 0))]
    x_args = [xc]
    if split_ctx:
        x_specs.append(pl.BlockSpec((1, TM, D_MODEL), lambda b, i: (b, 0, 0)))
        x_args.append(ctx_in)

    attn_specs = [pl.BlockSpec((1, TM, MLA_OUT), lambda b, i: (b, jnp.minimum(i, n_lat_tiles - 1), 0))]
    attn_args = [attn]
    if split_attn:
        attn_specs.append(pl.BlockSpec((1, TM, MLA_OUT), lambda b, i: (b, 0, 0)))
        attn_args.append(attn_ctx)

    def const(shape):
        return pl.BlockSpec(shape, lambda b, i: (0,) * len(shape), pipeline_mode=pl.Buffered(1))

    prev_spec = pl.BlockSpec((1, POOL_HALO, POOL_DIM), lambda b, i: (b, jnp.maximum(i * per - 1, 0), 0))
    next_spec = pl.BlockSpec((1, POOL_HALO, POOL_DIM),
                             lambda b, i: (b, jnp.minimum((i + 1) * per, halo_blocks - 1), 0))
    mod_spec = pl.BlockSpec((1, 6, D_MODEL), lambda b, i: (jnp.where(i >= n_lat_tiles, ctx_row, b), 0, 0))
    weights = [wos, woa, wop, pw, ps, n2, w1, w2] + ([fn] if final else [])
    return pl.pallas_call(
        functools.partial(_mix_mlp_kernel, final=final, n_lat_tiles=n_lat_tiles, n_tiles=n_tiles,
                          seq=seq, ctx=ctx, split_ctx=split_ctx, split_attn=split_attn),
        grid=(bsz, out_tiles),
        in_specs=x_specs + [tok(SSD_PAD)] + attn_specs + [tok(POOL_DIM), prev_spec, next_spec, mod_spec]
                 + [const(w.shape) for w in weights],
        out_specs=tok(D_MODEL),
        out_shape=jax.ShapeDtypeStruct((bsz, out_tiles * TM, D_MODEL), F32),
        scratch_shapes=[pltpu.VMEM((TM + 2 * POOL_HALO, POOL_DIM), F32)],
        compiler_params=pltpu.CompilerParams(dimension_semantics=("parallel", "parallel"),
                                             vmem_limit_bytes=VMEM_LIMIT),
        name="mix_mlp_final" if final else "mix_mlp",
    )(*x_args, ssd, *attn_args, pool, pool, pool, mod, *weights)


def kernel(x, c, ctx, c_ctx, mod_w, mod_b, norm1_w, norm2_w, w_in, conv_w, conv_b, dt_bias, a_log, ssd_d,
           ssd_norm_w, q_a_norm_w, w_q_b, kv_a_norm_w, w_kv_b, pool_w, pool_scale, w_out, w_mlp1, w_mlp2,
           final_norm_w):
    bsz, seq, _ = x.shape
    n_ctx = ctx.shape[1]
    depth = mod_w.shape[0]
    assert seq % TM == 0 and n_ctx % TM == 0 and seq % GRID_W == 0
    assert seq % (SSD_STEP_CHUNKS * SSD_CHUNK) == 0 and n_ctx % (SSD_STEP_CHUNKS * SSD_CHUNK) == 0
    assert seq % ATT_TQ_LATENT == 0 and n_ctx % ATT_TQ == 0 and seq % n_ctx == 0
    n_lat_tiles = seq // TM
    n_lat_chunks = seq // SSD_CHUNK

    assert n_ctx == TM
    xc, ctx_in = x, ctx
    cond_rows = -(-(bsz + 1) // 8) * 8
    cond = jnp.zeros((cond_rows, D_MODEL), F32).at[:bsz].set(c).at[bsz].set(c_ctx)
    mod_all = _modulation(cond, mod_w, mod_b)[:, :bsz + 1]
    rope = _rope_tables(seq, n_ctx)

    win_idx, wq_idx, wkv_idx = _win_index(), _wq_index(), _wkv_index()
    pad_idx = _pad_group_index(0)
    xbc_idx = np.concatenate([pad_idx, SSD_INNER + np.arange(XBC_DIM - SSD_INNER)])
    row = lambda v: v.reshape(1, -1)

    for i in range(depth):
        final = i == depth - 1
        win = _remap_cols(w_in[i], win_idx).astype(BF16)
        wq = _remap_cols(w_q_b[i], wq_idx).astype(BF16)
        wkv = _remap_cols(w_kv_b[i], wkv_idx).astype(BF16)
        convw = _remap_cols(conv_w[i], xbc_idx)
        convb = row(_remap_cols(conv_b[i], xbc_idx))
        dtb = row(jnp.pad(dt_bias[i].reshape(-1), (0, 128 - 2 * SSD_HEADS)))
        alog = row(jnp.pad(a_log[i].reshape(-1), (0, 128 - 2 * SSD_HEADS)))
        dskip = row(_remap_cols(jnp.repeat(ssd_d[i], SSD_HEAD_DIM), pad_idx))
        nw = row(_remap_cols(ssd_norm_w[i], pad_idx))
        wo = w_out[i].astype(BF16)
        wos = _remap_cols(wo[:SSD_INNER].T, pad_idx).T
        woa = wo[SSD_INNER:SSD_INNER + MLA_OUT]
        wop = wo[SSD_INNER + MLA_OUT:]
        pw = jax.scipy.linalg.block_diag(*[pool_w[i, g] for g in range(len(POOL_WINDOWS))]).astype(BF16)

        z, xs, bc, dt, pool, q, k, v = _inproj(xc, ctx_in, mod_all[i], row(norm1_w[i]), win, convw, convb,
                                                row(q_a_norm_w[i]), row(kv_a_norm_w[i]), wq, wkv, rope,
                                                n_lat_tiles)
        yb = _ssd_scan(xs, bc, dt, None, None, dtb, alog, None, None, n_lat_chunks, rev=True)
        ssd = _ssd_scan(xs, bc, dt, z, yb, dtb, alog, dskip, nw, n_lat_chunks, rev=False)
        attn = _attention(q, k, v, seq, n_ctx, latent=True)
        attn_ctx = None if final else _attention(q, k, v, seq, n_ctx, latent=False)
        xc = _mix_mlp(xc, ctx_in, ssd, attn, attn_ctx, pool, mod_all[i], wos, woa, wop, pw, row(pool_scale[i]),
                      row(norm2_w[i]), w_mlp1[i].astype(BF16), w_mlp2[i].astype(BF16), row(final_norm_w),
                      n_lat_tiles, seq, n_ctx, final)
        ctx_in = None
    return xc
```

```python
import functools
import math

import jax
import jax.numpy as jnp
import numpy as np
from jax import lax
from jax.experimental import pallas as pl
from jax.experimental.pallas import tpu as pltpu

F32 = jnp.float32
BF16 = jnp.bfloat16

D_MODEL = 1024
GRID_W = 64
EPS = 1e-6

SSD_HEADS = 6
SSD_HEAD_DIM = 64
SSD_GROUPS = 2
SSD_STATE = 128
SSD_CONV = 4
SSD_CHUNK = 128
SSD_STEP_CHUNKS = 2
HEADS_PER_GROUP = SSD_HEADS // SSD_GROUPS
GROUP_DIM = HEADS_PER_GROUP * SSD_HEAD_DIM
GROUP_PAD = 256
SSD_INNER = SSD_HEADS * SSD_HEAD_DIM
SSD_PAD = SSD_GROUPS * GROUP_PAD
XBC_DIM = SSD_INNER + 2 * SSD_GROUPS * SSD_STATE
XBC_PAD = SSD_PAD + 2 * SSD_GROUPS * SSD_STATE

MLA_HEADS = 6
Q_LORA = 256
KV_LORA = 256
QK_NOPE = 64
QK_ROPE = 32
V_HEAD = 64
QK_DIM = QK_NOPE + QK_ROPE
MLA_OUT = MLA_HEADS * V_HEAD
ROPE_THETA = 10000.0
ROPE_PAIRS = QK_ROPE // 4
HEAD_SLAB = 128

POOL_WINDOWS = (2, 4, 8, 16)
POOL_GROUP_DIM = 64
POOL_DIM = len(POOL_WINDOWS) * POOL_GROUP_DIM
POOL_HALO = 8
CONV_HALO = 8

D_FF = 4 * D_MODEL
FF_CHUNK = 1024

OFF_Z = 0
OFF_XBC = OFF_Z + SSD_INNER
OFF_DT = OFF_XBC + XBC_DIM
OFF_QA = OFF_DT + 2 * SSD_HEADS
OFF_KVA = OFF_QA + Q_LORA
OFF_KROPE = OFF_KVA + KV_LORA
OFF_POOL = OFF_KROPE + QK_ROPE
IN_COLS = OFF_POOL + POOL_DIM

P_Z = 0
P_XBC = P_Z + SSD_PAD
P_QA = P_XBC + XBC_PAD
P_KVA = P_QA + Q_LORA
P_POOL = P_KVA + KV_LORA
P_DT = P_POOL + POOL_DIM
P_KR = P_DT + 128
P_COLS = P_KR + HEAD_SLAB

TM = 256
ATT_TQ = 256
ATT_TQ_LATENT = 512
ATT_TK = 512
BOUND_TK = 256
BOUND_SLACK = 1.0 + 2.0 ** -9
MAX_BOUND_GAP = 64.0
VT_ROWS = 80
NEG_BIG = -1e30
VMEM_LIMIT = 56 * 1024 * 1024
LOG2E = math.log2(math.e)


def _pad_group_index(base):
    idx = np.full((SSD_PAD,), -1, np.int64)
    for g in range(SSD_GROUPS):
        idx[g * GROUP_PAD:g * GROUP_PAD + GROUP_DIM] = base + g * GROUP_DIM + np.arange(GROUP_DIM)
    return idx


def _rope_perm():
    src = np.zeros((QK_ROPE,), np.int64)
    for half in range(2):
        for axis in range(2):
            for p in range(ROPE_PAIRS):
                src[half * 16 + axis * 8 + p] = axis * 16 + half * 8 + p
    return src


def _win_index():
    idx = np.full((P_COLS,), -1, np.int64)
    idx[P_Z:P_Z + SSD_PAD] = _pad_group_index(OFF_Z)
    idx[P_XBC:P_XBC + SSD_PAD] = _pad_group_index(OFF_XBC)
    nbc = 2 * SSD_GROUPS * SSD_STATE
    idx[P_XBC + SSD_PAD:P_XBC + SSD_PAD + nbc] = OFF_XBC + SSD_INNER + np.arange(nbc)
    idx[P_QA:P_QA + Q_LORA] = OFF_QA + np.arange(Q_LORA)
    idx[P_KVA:P_KVA + KV_LORA] = OFF_KVA + np.arange(KV_LORA)
    idx[P_POOL:P_POOL + POOL_DIM] = OFF_POOL + np.arange(POOL_DIM)
    idx[P_DT:P_DT + 2 * SSD_HEADS] = OFF_DT + np.arange(2 * SSD_HEADS)
    idx[P_KR + QK_NOPE:P_KR + QK_NOPE + QK_ROPE] = OFF_KROPE + _rope_perm()
    return idx


def _wq_index():
    idx = np.full((MLA_HEADS * HEAD_SLAB,), -1, np.int64)
    perm = _rope_perm()
    for h in range(MLA_HEADS):
        idx[h * HEAD_SLAB:h * HEAD_SLAB + QK_NOPE] = h * QK_DIM + np.arange(QK_NOPE)
        idx[h * HEAD_SLAB + QK_NOPE:h * HEAD_SLAB + QK_DIM] = h * QK_DIM + QK_NOPE + perm
    return idx


def _wkv_index():
    nk = MLA_HEADS * HEAD_SLAB
    idx = np.full((nk + MLA_OUT,), -1, np.int64)
    per_head = QK_NOPE + V_HEAD
    for h in range(MLA_HEADS):
        idx[h * HEAD_SLAB:h * HEAD_SLAB + QK_NOPE] = h * per_head + np.arange(QK_NOPE)
        idx[nk + h * V_HEAD:nk + (h + 1) * V_HEAD] = h * per_head + QK_NOPE + np.arange(V_HEAD)
    return idx


def _remap_cols(w, idx):
    take = jnp.take(w, jnp.asarray(np.maximum(idx, 0)), axis=-1)
    return jnp.where(jnp.asarray(idx >= 0), take, jnp.zeros((), w.dtype))


def _rope_tables(seq, ctx):
    t = np.arange(seq)
    row = jnp.asarray((t // GRID_W).astype(np.float32))
    col = jnp.asarray((t % GRID_W).astype(np.float32))
    inv_freq = ROPE_THETA ** (-jnp.arange(ROPE_PAIRS, dtype=F32) / ROPE_PAIRS)
    ang = jnp.concatenate([row[:, None] * inv_freq, col[:, None] * inv_freq], axis=1)
    cos16, sin16 = jnp.cos(ang), jnp.sin(ang)
    ones = jnp.ones((seq, QK_NOPE), F32)
    zeros = jnp.zeros((seq, QK_NOPE), F32)
    z16 = jnp.zeros((seq, 16), F32)
    tail1 = jnp.ones((seq, HEAD_SLAB - QK_DIM), F32)
    tail0 = jnp.zeros((seq, HEAD_SLAB - QK_DIM), F32)
    c_tab = jnp.concatenate([ones, cos16, cos16, tail1], axis=1)
    s1_tab = jnp.concatenate([zeros, -sin16, z16, tail0], axis=1)
    s2_tab = jnp.concatenate([zeros, z16, sin16, tail0], axis=1)
    c_tab = jnp.concatenate([c_tab, jnp.ones((ctx, HEAD_SLAB), F32)], axis=0)
    s1_tab = jnp.concatenate([s1_tab, jnp.zeros((ctx, HEAD_SLAB), F32)], axis=0)
    s2_tab = jnp.concatenate([s2_tab, jnp.zeros((ctx, HEAD_SLAB), F32)], axis=0)
    k_tabs = jnp.stack([c_tab, s1_tab, s2_tab])
    qscale = (QK_DIM ** -0.5) * LOG2E
    return jnp.concatenate([k_tabs, k_tabs * qscale], axis=0)


def _rms(u, w_row):
    ms = jnp.mean(u * u, axis=-1, keepdims=True)
    return u * lax.rsqrt(ms + EPS) * w_row


def _silu(u):
    return u * (1.0 / (1.0 + jnp.exp(-u)))


def _softplus(u):
    return jnp.maximum(u, 0.0) + jnp.log1p(jnp.exp(-jnp.abs(u)))


def _mod_kernel(cond_ref, w_ref, b_ref, o_ref):
    cnd = cond_ref[...]
    o_ref[0] = jnp.dot(_silu(cnd), w_ref[0], preferred_element_type=F32,
                       precision=lax.Precision.HIGHEST) + b_ref[0]


def _modulation(cond, mod_w, mod_b):
    depth = mod_w.shape[0]
    r = cond.shape[0]
    out = pl.pallas_call(
        _mod_kernel,
        grid=(depth, 6),
        in_specs=[pl.BlockSpec((r, D_MODEL), lambda d, j: (0, 0)),
                  pl.BlockSpec((1, D_MODEL, D_MODEL), lambda d, j: (d, 0, j)),
                  pl.BlockSpec((1, 1, D_MODEL), lambda d, j: (d, 0, j))],
        out_specs=pl.BlockSpec((1, r, D_MODEL), lambda d, j: (d, 0, j)),
        out_shape=jax.ShapeDtypeStruct((depth, r, 6 * D_MODEL), F32),
        name="adaln_modulation",
    )(cond, mod_w, mod_b.reshape(depth, 1, 6 * D_MODEL))
    return out.reshape(depth, r, 6, D_MODEL)


def _rope(t, c, s1, s2):
    return t * c + pltpu.roll(t, HEAD_SLAB - 16, 1) * s1 + pltpu.roll(t, 16, 1) * s2


def _token_tile(x_ref, ctx_ref, tile, n_lat_tiles):
    if ctx_ref is None:
        return x_ref[0]
    return jnp.where(tile >= n_lat_tiles, ctx_ref[0], x_ref[0])


def _inproj_kernel(*refs, n_lat_tiles, n_tiles, split_ctx):
    x_ref, ctx_ref = (refs[0], refs[1]) if split_ctx else (refs[0], None)
    (xprev_ref, xnext_ref, mod_ref, n1_ref, win_ref, convw_ref, convb_ref, qan_ref, kvan_ref, wq_ref, wkv_ref,
     rope_ref, z_ref, xs_ref, bc_ref, dt_ref, pool_ref, q_ref, k_ref, vt_ref, kn_ref,
     ext_scr) = refs[2 if split_ctx else 1:]

    def norm_mod(x):
        return (_rms(x, n1_ref[...]) * (1.0 + mod_ref[0, 1:2, :]) + mod_ref[0, 0:1, :]).astype(BF16)

    hb = norm_mod(_token_tile(x_ref, ctx_ref, pl.program_id(1), n_lat_tiles))

    def proj(lo, width):
        return jnp.dot(hb, win_ref[:, lo:lo + width], preferred_element_type=F32)

    z_ref[0] = proj(P_Z, SSD_PAD)
    pool_ref[0] = proj(P_POOL, POOL_DIM)

    tile = pl.program_id(1)
    seg_first = jnp.logical_or(tile == 0, tile == n_lat_tiles)
    seg_last = jnp.logical_or(tile == n_lat_tiles - 1, tile == n_tiles - 1)
    halo = norm_mod(jnp.concatenate([xprev_ref[0], xnext_ref[0]], axis=0))
    xbc = jnp.dot(jnp.concatenate([hb, halo], axis=0), win_ref[:, P_XBC:P_XBC + XBC_PAD],
                  preferred_element_type=F32)
    ext_scr[0:CONV_HALO, :] = xbc[TM:TM + CONV_HALO] * jnp.where(seg_first, 0.0, 1.0)
    ext_scr[CONV_HALO:CONV_HALO + TM, :] = xbc[0:TM]
    ext_scr[CONV_HALO + TM:, :] = xbc[TM + CONV_HALO:] * jnp.where(seg_last, 0.0, 1.0)
    u = convb_ref[...]
    for tap in range(SSD_CONV):
        lo = CONV_HALO - 1 + tap
        u = u + ext_scr[lo:lo + TM, :] * convw_ref[tap:tap + 1, :]
    u = _silu(u)
    xs_ref[0] = u[:, :SSD_PAD]
    bc_ref[0] = u[:, SSD_PAD:].astype(BF16)
    dt_ref[0] = proj(P_DT, 128)

    cq = _rms(proj(P_QA, Q_LORA), qan_ref[...]).astype(BF16)
    ckv = _rms(proj(P_KVA, KV_LORA), kvan_ref[...]).astype(BF16)
    q = jnp.dot(cq, wq_ref[...], preferred_element_type=F32)
    kv = jnp.dot(ckv, wkv_ref[...], preferred_element_type=F32)
    k_rope = _rope(proj(P_KR, HEAD_SLAB), rope_ref[0], rope_ref[1], rope_ref[2])
    knorm_rows = []
    for hd in range(MLA_HEADS):
        sl = slice(hd * HEAD_SLAB, (hd + 1) * HEAD_SLAB)
        q_ref[0, :, sl] = _rope(q[:, sl], rope_ref[3], rope_ref[4], rope_ref[5]).astype(BF16)
        k_b = (kv[:, sl] + k_rope).astype(BF16)
        k_ref[0, :, sl] = k_b
        k_f = k_b.astype(F32)
        n2 = jnp.max(jnp.sum(k_f * k_f, axis=-1, keepdims=True), axis=0, keepdims=True)
        knorm_rows.append(jnp.broadcast_to(n2, (1, HEAD_SLAB)))
    knorm_rows.append(jnp.zeros((8 - MLA_HEADS, HEAD_SLAB), F32))
    kn_ref[0, 0] = jnp.concatenate(knorm_rows, axis=0)
    nk = MLA_HEADS * HEAD_SLAB
    extra = VT_ROWS - V_HEAD
    ones_rows = (lax.broadcasted_iota(jnp.int32, (extra, TM), 0) == 0).astype(BF16)
    for pair in range(MLA_HEADS // 2):
        vt = kv[:, nk + pair * HEAD_SLAB:nk + (pair + 1) * HEAD_SLAB].T.astype(BF16)
        for sub in range(2):
            base = (2 * pair + sub) * VT_ROWS
            vt_ref[0, base:base + V_HEAD, :] = vt[sub * V_HEAD:(sub + 1) * V_HEAD, :]
            vt_ref[0, base + V_HEAD:base + VT_ROWS, :] = ones_rows


def _inproj(xc, ctx_in, mod, n1, win, convw, convb, qan, kvan, wq, wkv, rope, n_lat_tiles):
    bsz = xc.shape[0]
    split_ctx = ctx_in is not None
    t = xc.shape[1] + (ctx_in.shape[1] if split_ctx else 0)
    nt = t // TM
    ctx_row = mod.shape[0] - 1
    halo_blocks = xc.shape[1] // CONV_HALO
    last_x_tile = xc.shape[1] // TM - 1
    per = TM // CONV_HALO

    def tok(width):
        return pl.BlockSpec((1, TM, width), lambda b, i: (b, i, 0))

    x_specs = [pl.BlockSpec((1, TM, D_MODEL), lambda b, i: (b, jnp.minimum(i, last_x_tile), 0))]
    x_args = [xc]
    if split_ctx:
        x_specs.append(pl.BlockSpec((1, TM, D_MODEL), lambda b, i: (b, 0, 0)))
        x_args.append(ctx_in)

    def const(shape):
        return pl.BlockSpec(shape, lambda b, i: (0,) * len(shape))

    prev_spec = pl.BlockSpec((1, CONV_HALO, D_MODEL), lambda b, i: (b, jnp.maximum(i * per - 1, 0), 0))
    next_spec = pl.BlockSpec((1, CONV_HALO, D_MODEL),
                             lambda b, i: (b, jnp.minimum((i + 1) * per, halo_blocks - 1), 0))
    mod_spec = pl.BlockSpec((1, 6, D_MODEL), lambda b, i: (jnp.where(i >= n_lat_tiles, ctx_row, b), 0, 0))
    nh = MLA_HEADS * HEAD_SLAB
    nbc = 2 * SSD_GROUPS * SSD_STATE
    outs = [(SSD_PAD, F32), (SSD_PAD, F32), (nbc, BF16), (128, F32), (POOL_DIM, F32), (nh, BF16), (nh, BF16)]
    vt_rows = MLA_HEADS * VT_ROWS
    return pl.pallas_call(
        functools.partial(_inproj_kernel, n_lat_tiles=n_lat_tiles, n_tiles=nt, split_ctx=split_ctx),
        grid=(bsz, nt),
        in_specs=x_specs + [prev_spec, next_spec, mod_spec, const((1, D_MODEL)), const((D_MODEL, P_COLS)),
                  const(convw.shape), const(convb.shape),
                  const((1, Q_LORA)), const((1, KV_LORA)), const((Q_LORA, nh)), const((KV_LORA, nh + MLA_OUT)),
                  pl.BlockSpec((6, TM, HEAD_SLAB), lambda b, i: (0, i, 0))],
        out_specs=[tok(w) for w, _ in outs] + [pl.BlockSpec((1, vt_rows, TM), lambda b, i: (b, 0, i)),
                                                pl.BlockSpec((1, 1, 8, HEAD_SLAB), lambda b, i: (b, i, 0, 0))],
        out_shape=[jax.ShapeDtypeStruct((bsz, t, w), dt) for w, dt in outs]
                  + [jax.ShapeDtypeStruct((bsz, vt_rows, t), BF16),
                     jax.ShapeDtypeStruct((bsz, nt, 8, HEAD_SLAB), F32)],
        scratch_shapes=[pltpu.VMEM((TM + 2 * CONV_HALO, XBC_PAD), F32)],
        compiler_params=pltpu.CompilerParams(dimension_semantics=("parallel", "parallel"),
                                             vmem_limit_bytes=VMEM_LIMIT),
        name="inproj_mla",
    )(*x_args, xc, xc, mod, n1, win, convw, convb, qan, kvan, wq, wkv, rope)


def _expand_heads(cols, first_lane):
    n = cols.shape[0]
    lane = lax.broadcasted_iota(jnp.int32, (n, 128), 1)
    parts = []
    for g in range(SSD_GROUPS):
        b = [jnp.broadcast_to(cols[:, first_lane + HEADS_PER_GROUP * g + i:first_lane + HEADS_PER_GROUP * g + i + 1],
                              (n, 128)) for i in range(HEADS_PER_GROUP)]
        parts.append(jnp.where(lane < SSD_HEAD_DIM, b[0], b[1]))
        parts.append(b[2])
    return jnp.concatenate(parts, axis=1)


def _ssd_kernel(*refs, rev):
    h_scr = refs[-1]

    @pl.when(pl.program_id(1) == 0)
    def _():
        h_scr[...] = jnp.zeros_like(h_scr)

    order = range(SSD_STEP_CHUNKS - 1, -1, -1) if rev else range(SSD_STEP_CHUNKS)
    for sub in order:
        _ssd_one_chunk(refs, rev, slice(sub * SSD_CHUNK, (sub + 1) * SSD_CHUNK))


def _ssd_one_chunk(refs, rev, rows):
    if rev:
        xs_ref, bc_ref, dt_ref, dtb_ref, alog_ref, out_ref, h_scr = refs
    else:
        xs_ref, bc_ref, dt_ref, z_ref, yb_ref, dtb_ref, alog_ref, dskip_ref, nw_ref, out_ref, h_scr = refs
    L = SSD_CHUNK
    xs = xs_ref[0, rows, :]
    bmat = bc_ref[0, rows, 0:SSD_GROUPS * SSD_STATE]
    cmat = bc_ref[0, rows, SSD_GROUPS * SSD_STATE:]

    first_lane = SSD_HEADS if rev else 0
    dtv = _softplus(dt_ref[0, rows, :] + dtb_ref[...])
    adt = dtv * (-jnp.exp(alog_ref[...]))
    ri = lax.broadcasted_iota(jnp.int32, (L, L), 0)
    ci = lax.broadcasted_iota(jnp.int32, (L, L), 1)
    causal = (ci >= ri) if rev else (ci <= ri)
    cs_col = jnp.dot(causal.astype(F32), adt, preferred_element_type=F32,
                     precision=lax.Precision.HIGHEST)
    cs_row = cs_col.T

    dt_e = _expand_heads(dtv, first_lane)
    cs_e = _expand_heads(cs_col, first_lane)
    end_row = 0 if rev else L - 1
    cs_end = cs_e[end_row:end_row + 1, :]
    xdt = xs * dt_e
    xdt_b = xdt.astype(BF16)
    xdd_b = (xdt * jnp.exp(cs_end - cs_e)).astype(BF16)
    in_decay = jnp.exp(cs_e)
    state_decay = jnp.exp(cs_end)

    lane_g = lax.broadcasted_iota(jnp.int32, (1, GROUP_PAD), 1)
    ys = []
    for g in range(SSD_GROUPS):
        gs = slice(g * GROUP_PAD, (g + 1) * GROUP_PAD)
        ns = slice(g * SSD_STATE, (g + 1) * SSD_STATE)
        cm_g, bm_g = cmat[:, ns], bmat[:, ns]
        cb = lax.dot_general(cm_g, bm_g, (((1,), (1,)), ((), ())), preferred_element_type=F32)
        x_g = xdt_b[:, gs]
        y_g = jnp.zeros((L, GROUP_PAD), F32)
        for i in range(HEADS_PER_GROUP):
            hl = first_lane + HEADS_PER_GROUP * g + i
            diff = cs_col[:, hl:hl + 1] - cs_row[hl:hl + 1, :]
            decay = jnp.exp(jnp.where(causal, diff, NEG_BIG))
            head_lanes = jnp.logical_and(lane_g >= i * SSD_HEAD_DIM, lane_g < (i + 1) * SSD_HEAD_DIM)
            x_h = jnp.where(head_lanes, x_g, jnp.zeros((), BF16))
            y_g = y_g + jnp.dot((cb * decay).astype(BF16), x_h, preferred_element_type=F32)
        h_t = h_scr[g]
        y_in = jnp.dot(cm_g, h_t.astype(BF16), preferred_element_type=F32)
        y_g = y_g + y_in * in_decay[:, gs]
        s_new = lax.dot_general(bm_g, xdd_b[:, gs], (((0,), (0,)), ((), ())), preferred_element_type=F32)
        h_scr[g] = h_t * state_decay[:, gs] + s_new
        ys.append(y_g)
    y = jnp.concatenate(ys, axis=1)

    if rev:
        out_ref[0, rows, :] = y
    else:
        y = y + yb_ref[0, rows, :] + xs * dskip_ref[...]
        gated = y * _silu(z_ref[0, rows, :])
        outs = []
        for g in range(SSD_GROUPS):
            sl = gated[:, g * GROUP_PAD:(g + 1) * GROUP_PAD]
            ms = jnp.sum(sl * sl, axis=-1, keepdims=True) * (1.0 / GROUP_DIM)
            outs.append(sl * lax.rsqrt(ms + EPS))
        out_ref[0, rows, :] = (jnp.concatenate(outs, axis=1) * nw_ref[...]).astype(BF16)


def _ssd_tile(j, rev, n_lat_tiles, n_tiles):
    if rev:
        return n_tiles - 1 - j
    n_ctx = n_tiles - n_lat_tiles
    return jnp.where(j < n_ctx, n_lat_tiles + j, j - n_ctx)


def _ssd_scan(xs, bc, dt, z, yb, dtb, alog, dskip, nw, n_lat_chunks, rev):
    bsz, t, _ = xs.shape
    step_rows = SSD_STEP_CHUNKS * SSD_CHUNK
    n_chunks = t // step_rows
    chunk = functools.partial(_ssd_tile, rev=rev, n_lat_tiles=n_lat_chunks // SSD_STEP_CHUNKS, n_tiles=n_chunks)

    def tok(width):
        return pl.BlockSpec((1, step_rows, width), lambda b, j: (b, chunk(j), 0))

    def const(shape):
        return pl.BlockSpec(shape, lambda b, j: (0,) * len(shape))

    params = [dtb, alog]
    param_specs = [const(p.shape) for p in params]
    if rev:
        args = [xs, bc, dt] + params
        in_specs = [tok(SSD_PAD), tok(bc.shape[-1]), tok(128)] + param_specs
        out_dtype = F32
    else:
        args = [xs, bc, dt, z, yb] + params + [dskip, nw]
        in_specs = ([tok(SSD_PAD), tok(bc.shape[-1]), tok(128), tok(SSD_PAD), tok(SSD_PAD)] + param_specs
                    + [const(dskip.shape), const(nw.shape)])
        out_dtype = BF16
    return pl.pallas_call(
        functools.partial(_ssd_kernel, rev=rev),
        grid=(bsz, n_chunks),
        in_specs=in_specs,
        out_specs=tok(SSD_PAD),
        out_shape=jax.ShapeDtypeStruct((bsz, t, SSD_PAD), out_dtype),
        scratch_shapes=[pltpu.VMEM((SSD_GROUPS, SSD_STATE, GROUP_PAD), F32)],
        compiler_params=pltpu.CompilerParams(dimension_semantics=("parallel", "arbitrary"),
                                             vmem_limit_bytes=VMEM_LIMIT),
        name="ssd_bwd" if rev else "ssd_fwd",
    )(*args)


def _attn_kernel(*refs):
    q_ref, k_ref, vt_ref, kn_ref, o_ref, s_scr = refs
    tq = q_ref.shape[1]
    n_keys = k_ref.shape[1]
    chunks = [(lo, min(ATT_TK, n_keys - lo)) for lo in range(0, n_keys, ATT_TK)]
    mblk = max(8, min(ATT_TK, n_keys, (8 * 1024) // tq))

    def scores(hd, lo, size, m_acc):
        sl = slice(hd * HEAD_SLAB, (hd + 1) * HEAD_SLAB)
        s_c = lax.dot_general(k_ref[0, lo:lo + size, sl], q_ref[0, :, sl], (((1,), (1,)), ((), ())),
                              preferred_element_type=F32)
        s_scr[hd % 2, lo:lo + size, :] = s_c
        for r in range(0, size, mblk):
            blk = s_scr[hd % 2, lo + r:lo + r + mblk, :]
            m_acc = blk if m_acc is None else jnp.maximum(m_acc, blk)
        return m_acc

    def column_max(m_acc):
        rows = m_acc.shape[0]
        while rows > 8:
            rows //= 2
            m_acc = jnp.maximum(m_acc[0:rows], m_acc[rows:2 * rows])
        return jnp.max(m_acc, axis=0, keepdims=True)

    def finish(acc):
        return acc[0:V_HEAD] * (1.0 / acc[V_HEAD:V_HEAD + 1])

    def write_out(out_t):
        for pair in range(MLA_HEADS // 2):
            o_ref[0, :, pair * HEAD_SLAB:(pair + 1) * HEAD_SLAB] = (
                jnp.concatenate(out_t[2 * pair:2 * pair + 2], axis=0).T.astype(BF16))

    def bounded_path():
        k_max2 = jnp.max(kn_ref[0], axis=0)
        ones = jnp.ones((8, HEAD_SLAB), F32)
        small = [(lo, min(BOUND_TK, n_keys - lo)) for lo in range(0, n_keys, BOUND_TK)]
        slots = [(hd, lo, size) for hd in range(MLA_HEADS) for lo, size in small]

        def score_chunk(hd, lo, size):
            sl = slice(hd * HEAD_SLAB, (hd + 1) * HEAD_SLAB)
            return lax.dot_general(k_ref[0, lo:lo + size, sl], q_ref[0, :, sl], (((1,), (1,)), ((), ())),
                                   preferred_element_type=F32)

        bounds = []
        for hd in range(MLA_HEADS):
            q_f = q_ref[0, :, hd * HEAD_SLAB:(hd + 1) * HEAD_SLAB].astype(F32)
            q_n2 = lax.dot_general(ones, q_f * q_f, (((1,), (1,)), ((), ())), preferred_element_type=F32,
                                   precision=lax.Precision.HIGHEST)[0:1]
            bounds.append(jnp.sqrt(q_n2 * k_max2[hd:hd + 1, 0:1]) * BOUND_SLACK)

        out_t = []
        worst = None
        s_next = score_chunk(*slots[0])
        for idx, (hd, lo, size) in enumerate(slots):
            if lo == 0:
                bound = bounds[hd]
                m_acc = None
                acc = jnp.zeros((VT_ROWS, tq), F32)
            s_c = s_next
            if idx + 1 < len(slots):
                s_next = score_chunk(*slots[idx + 1])
            for r in range(0, size, mblk):
                blk = s_c[r:r + mblk]
                m_acc = blk if m_acc is None else jnp.maximum(m_acc, blk)
            acc = acc + jnp.dot(vt_ref[0, hd * VT_ROWS:(hd + 1) * VT_ROWS, lo:lo + size],
                                jnp.exp2(s_c - bound).astype(BF16), preferred_element_type=F32)
            if lo + size == n_keys:
                out_t.append(finish(acc))
                gap = bound - column_max(m_acc)
                worst = gap if worst is None else jnp.maximum(worst, gap)
        write_out(out_t)
        return jnp.max(worst)

    def exact_path():
        out_t = []
        m_prev = None
        for hd in range(MLA_HEADS + 1):
            m_acc = None
            acc = jnp.zeros((VT_ROWS, tq), F32)
            for lo, size in chunks:
                if hd < MLA_HEADS:
                    m_acc = scores(hd, lo, size, m_acc)
                if hd > 0:
                    p_t = jnp.exp2(s_scr[(hd - 1) % 2, lo:lo + size, :] - m_prev).astype(BF16)
                    acc = acc + jnp.dot(vt_ref[0, (hd - 1) * VT_ROWS:hd * VT_ROWS, lo:lo + size], p_t,
                                        preferred_element_type=F32)
            if hd > 0:
                out_t.append(finish(acc))
            if hd < MLA_HEADS:
                m_prev = column_max(m_acc)
        write_out(out_t)

    worst_gap = bounded_path()

    @pl.when(jnp.logical_not(worst_gap <= MAX_BOUND_GAP))
    def _():
        exact_path()


def _attention(q, k, vt, kn, seq, ctx, latent):
    bsz, t, _ = q.shape
    n_kn = kn.shape[1]
    assert ctx == TM
    tq = ATT_TQ_LATENT if latent else ATT_TQ
    n_tiles = (seq if latent else ctx) // tq
    q_off = 0 if latent else seq // tq
    n_keys = t if latent else ctx
    key_blk = 0 if latent else seq // ctx
    nh = MLA_HEADS * HEAD_SLAB
    in_specs = [pl.BlockSpec((1, tq, nh), lambda b, i: (b, q_off + i, 0)),
                pl.BlockSpec((1, n_keys, nh), lambda b, i: (b, key_blk, 0)),
                pl.BlockSpec((1, MLA_HEADS * VT_ROWS, n_keys), lambda b, i: (b, 0, key_blk)),
                pl.BlockSpec((1, n_kn, 8, HEAD_SLAB), lambda b, i: (b, 0, 0, 0)) if latent else
                pl.BlockSpec((1, 1, 8, HEAD_SLAB), lambda b, i: (b, n_kn - 1, 0, 0))]
    return pl.pallas_call(
        _attn_kernel,
        grid=(bsz, n_tiles),
        in_specs=in_specs,
        out_specs=pl.BlockSpec((1, tq, MLA_OUT), lambda b, i: (b, i, 0)),
        out_shape=jax.ShapeDtypeStruct((bsz, n_tiles * tq, MLA_OUT), BF16),
        scratch_shapes=[pltpu.VMEM((2, n_keys, tq), F32)],
        compiler_params=pltpu.CompilerParams(dimension_semantics=("parallel", "arbitrary"),
                                             vmem_limit_bytes=VMEM_LIMIT),
        name="mla_attention" if latent else "mla_attention_ctx",
    )(q, k, vt, kn)


def _pool_mix(pool_ref, pprev_ref, pnext_ref, ext_scr, pw_ref, ps_ref, tile, n_lat_tiles, n_tiles, seq, ctx):
    seg_first = jnp.logical_or(tile == 0, tile == n_lat_tiles)
    seg_last = jnp.logical_or(tile == n_lat_tiles - 1, tile == n_tiles - 1)
    h0 = POOL_HALO
    ext_scr[0:h0, :] = pprev_ref[0] * jnp.where(seg_first, 0.0, 1.0)
    ext_scr[h0:h0 + TM, :] = pool_ref[0]
    ext_scr[h0 + TM:h0 + TM + h0, :] = pnext_ref[0] * jnp.where(seg_last, 0.0, 1.0)

    is_ctx = tile >= n_lat_tiles
    seg_len = jnp.where(is_ctx, ctx, seq)
    pos = lax.broadcasted_iota(jnp.int32, (TM, 1), 0) + jnp.where(is_ctx, tile - n_lat_tiles, tile) * TM
    lane = lax.broadcasted_iota(jnp.int32, (1, 128), 1)
    low_half = lane < POOL_GROUP_DIM

    def inv_count(w):
        cnt = jnp.minimum(pos + (w - w // 2), seg_len) - jnp.maximum(pos - w // 2, 0)
        return 1.0 / cnt.astype(F32)

    def taps(slab, offsets):
        tot = None
        for k in offsets:
            piece = ext_scr[h0 + k:h0 + k + TM, slab * 128:(slab + 1) * 128]
            tot = piece if tot is None else tot + piece
        return tot

    outs = []
    for slab in range(2):
        w_small, w_big = POOL_WINDOWS[2 * slab], POOL_WINDOWS[2 * slab + 1]
        small = taps(slab, range(-(w_small // 2), w_small - w_small // 2))
        extra = [k for k in range(-(w_big // 2), w_big - w_big // 2)
                 if not -(w_small // 2) <= k < w_small - w_small // 2]
        big = small + taps(slab, extra)
        mean = jnp.where(low_half, small * inv_count(w_small), big * inv_count(w_big))
        outs.append(mean - ext_scr[h0:h0 + TM, slab * 128:(slab + 1) * 128])
    d = jnp.concatenate(outs, axis=1).astype(BF16)
    return jnp.dot(d, pw_ref[...], preferred_element_type=F32) * ps_ref[...]


def _mix_mlp_kernel(*refs, final, n_lat_tiles, n_tiles, seq, ctx, split_ctx, split_attn):
    x_ref, ctx_ref = (refs[0], refs[1]) if split_ctx else (refs[0], None)
    refs = refs[2 if split_ctx else 1:]
    ssd_ref, attn_ref = refs[0], refs[1]
    attn_ctx_ref = refs[2] if split_attn else None
    refs = refs[3 if split_attn else 2:]
    if final:
        (pool_ref, pprev_ref, pnext_ref, mod_ref, wos_ref, woa_ref, wop_ref,
         pw_ref, ps_ref, n2_ref, w1_ref, w2_ref, fn_ref, o_ref, ext_scr) = refs
    else:
        (pool_ref, pprev_ref, pnext_ref, mod_ref, wos_ref, woa_ref, wop_ref,
         pw_ref, ps_ref, n2_ref, w1_ref, w2_ref, o_ref, ext_scr) = refs
    tile = pl.program_id(1)
    pool_y = _pool_mix(pool_ref, pprev_ref, pnext_ref, ext_scr, pw_ref, ps_ref, tile,
                       n_lat_tiles, n_tiles, seq, ctx)
    mix = (jnp.dot(ssd_ref[0], wos_ref[...], preferred_element_type=F32)
           + jnp.dot(_token_tile(attn_ref, attn_ctx_ref, tile, n_lat_tiles), woa_ref[...],
                     preferred_element_type=F32)
           + jnp.dot(pool_y.astype(BF16), wop_ref[...], preferred_element_type=F32))
    x1 = _token_tile(x_ref, ctx_ref, tile, n_lat_tiles) + mod_ref[0, 2:3, :] * mix
    h = _rms(x1, n2_ref[...])
    hb = (h * (1.0 + mod_ref[0, 4:5, :]) + mod_ref[0, 3:4, :]).astype(BF16)
    acc = jnp.zeros((TM, D_MODEL), F32)
    for lo in range(0, D_FF, FF_CHUNK):
        a = jnp.maximum(jnp.dot(hb, w1_ref[:, lo:lo + FF_CHUNK], preferred_element_type=F32), 0.0)
        acc = acc + jnp.dot((a * a).astype(BF16), w2_ref[lo:lo + FF_CHUNK, :], preferred_element_type=F32)
    x2 = x1 + mod_ref[0, 5:6, :] * acc
    if final:
        x2 = _rms(x2, fn_ref[...])
    o_ref[0] = x2


def _mix_mlp(xc, ctx_in, ssd, attn, attn_ctx, pool, mod, wos, woa, wop, pw, ps, n2, w1, w2, fn, n_lat_tiles,
             seq, ctx, final):
    bsz, t, _ = ssd.shape
    split_ctx = ctx_in is not None
    split_attn = attn_ctx is not None
    assert final or split_attn
    n_tiles = t // TM
    out_tiles = n_lat_tiles if final else n_tiles
    halo_blocks = t // POOL_HALO
    per = TM // POOL_HALO
    ctx_row = mod.shape[0] - 1
    last_x_tile = xc.shape[1] // TM - 1

    def tok(width):
        return pl.BlockSpec((1, TM, width), lambda b, i: (b, i, 0))

    x_specs = [pl.BlockSpec((1, TM, D_MODEL), lambda b, i: (b, jnp.minimum(i, last_x_tile), 0))]
    x_args = [xc]
    if split_ctx:
        x_specs.append(pl.BlockSpec((1, TM, D_MODEL), lambda b, i: (b, 0, 0)))
        x_args.append(ctx_in)

    attn_specs = [pl.BlockSpec((1, TM, MLA_OUT), lambda b, i: (b, jnp.minimum(i, n_lat_tiles - 1), 0))]
    attn_args = [attn]
    if split_attn:
        attn_specs.append(pl.BlockSpec((1, TM, MLA_OUT), lambda b, i: (b, 0, 0)))
        attn_args.append(attn_ctx)

    def const(shape):
        return pl.BlockSpec(shape, lambda b, i: (0,) * len(shape), pipeline_mode=pl.Buffered(1))

    prev_spec = pl.BlockSpec((1, POOL_HALO, POOL_DIM), lambda b, i: (b, jnp.maximum(i * per - 1, 0), 0))
    next_spec = pl.BlockSpec((1, POOL_HALO, POOL_DIM),
                             lambda b, i: (b, jnp.minimum((i + 1) * per, halo_blocks - 1), 0))
    mod_spec = pl.BlockSpec((1, 6, D_MODEL), lambda b, i: (jnp.where(i >= n_lat_tiles, ctx_row, b), 0, 0))
    weights = [wos, woa, wop, pw, ps, n2, w1, w2] + ([fn] if final else [])
    return pl.pallas_call(
        functools.partial(_mix_mlp_kernel, final=final, n_lat_tiles=n_lat_tiles, n_tiles=n_tiles,
                          seq=seq, ctx=ctx, split_ctx=split_ctx, split_attn=split_attn),
        grid=(bsz, out_tiles),
        in_specs=x_specs + [tok(SSD_PAD)] + attn_specs + [tok(POOL_DIM), prev_spec, next_spec, mod_spec]
                 + [const(w.shape) for w in weights],
        out_specs=tok(D_MODEL),
        out_shape=jax.ShapeDtypeStruct((bsz, out_tiles * TM, D_MODEL), F32),
        scratch_shapes=[pltpu.VMEM((TM + 2 * POOL_HALO, POOL_DIM), F32)],
        compiler_params=pltpu.CompilerParams(dimension_semantics=("parallel", "parallel"),
                                             vmem_limit_bytes=VMEM_LIMIT),
        name="mix_mlp_final" if final else "mix_mlp",
    )(*x_args, ssd, *attn_args, pool, pool, pool, mod, *weights)


def kernel(x, c, ctx, c_ctx, mod_w, mod_b, norm1_w, norm2_w, w_in, conv_w, conv_b, dt_bias, a_log, ssd_d,
           ssd_norm_w, q_a_norm_w, w_q_b, kv_a_norm_w, w_kv_b, pool_w, pool_scale, w_out, w_mlp1, w_mlp2,
           final_norm_w):
    bsz, seq, _ = x.shape
    n_ctx = ctx.shape[1]
    depth = mod_w.shape[0]
    assert seq % TM == 0 and n_ctx % TM == 0 and seq % GRID_W == 0
    assert seq % (SSD_STEP_CHUNKS * SSD_CHUNK) == 0 and n_ctx % (SSD_STEP_CHUNKS * SSD_CHUNK) == 0
    assert seq % ATT_TQ_LATENT == 0 and n_ctx % ATT_TQ == 0 and seq % n_ctx == 0
    n_lat_tiles = seq // TM
    n_lat_chunks = seq // SSD_CHUNK

    assert n_ctx == TM
    xc, ctx_in = x, ctx
    cond_rows = -(-(bsz + 1) // 8) * 8
    cond = jnp.zeros((cond_rows, D_MODEL), F32).at[:bsz].set(c).at[bsz].set(c_ctx)
    mod_all = _modulation(cond, mod_w, mod_b)[:, :bsz + 1]
    rope = _rope_tables(seq, n_ctx)

    win_idx, wq_idx, wkv_idx = _win_index(), _wq_index(), _wkv_index()
    pad_idx = _pad_group_index(0)
    xbc_idx = np.concatenate([pad_idx, SSD_INNER + np.arange(XBC_DIM - SSD_INNER)])
    row = lambda v: v.reshape(1, -1)

    for i in range(depth):
        final = i == depth - 1
        win = _remap_cols(w_in[i], win_idx).astype(BF16)
        wq = _remap_cols(w_q_b[i], wq_idx).astype(BF16)
        wkv = _remap_cols(w_kv_b[i], wkv_idx).astype(BF16)
        convw = _remap_cols(conv_w[i], xbc_idx)
        convb = row(_remap_cols(conv_b[i], xbc_idx))
        dtb = row(jnp.pad(dt_bias[i].reshape(-1), (0, 128 - 2 * SSD_HEADS)))
        alog = row(jnp.pad(a_log[i].reshape(-1), (0, 128 - 2 * SSD_HEADS)))
        dskip = row(_remap_cols(jnp.repeat(ssd_d[i], SSD_HEAD_DIM), pad_idx))
        nw = row(_remap_cols(ssd_norm_w[i], pad_idx))
        wo = w_out[i].astype(BF16)
        wos = _remap_cols(wo[:SSD_INNER].T, pad_idx).T
        woa = wo[SSD_INNER:SSD_INNER + MLA_OUT]
        wop = wo[SSD_INNER + MLA_OUT:]
        pw = jax.scipy.linalg.block_diag(*[pool_w[i, g] for g in range(len(POOL_WINDOWS))]).astype(BF16)

        z, xs, bc, dt, pool, q, k, v, kn = _inproj(xc, ctx_in, mod_all[i], row(norm1_w[i]), win, convw, convb,
                                                    row(q_a_norm_w[i]), row(kv_a_norm_w[i]), wq, wkv, rope,
                                                    n_lat_tiles)
        yb = _ssd_scan(xs, bc, dt, None, None, dtb, alog, None, None, n_lat_chunks, rev=True)
        ssd = _ssd_scan(xs, bc, dt, z, yb, dtb, alog, dskip, nw, n_lat_chunks, rev=False)
        attn = _attention(q, k, v, kn, seq, n_ctx, latent=True)
        attn_ctx = None if final else _attention(q, k, v, kn, seq, n_ctx, latent=False)
        xc = _mix_mlp(xc, ctx_in, ssd, attn, attn_ctx, pool, mod_all[i], wos, woa, wop, pw, row(pool_scale[i]),
                      row(norm2_w[i]), w_mlp1[i].astype(BF16), w_mlp2[i].astype(BF16), row(final_norm_w),
                      n_lat_tiles, seq, n_ctx, final)
        ctx_in = None
    return xc
```

```python
import functools
import math

import jax
import jax.numpy as jnp
import numpy as np
from jax import lax
from jax.experimental import pallas as pl
from jax.experimental.pallas import tpu as pltpu

F32 = jnp.float32
BF16 = jnp.bfloat16

D_MODEL = 1024
GRID_W = 64
EPS = 1e-6

SSD_HEADS = 6
SSD_HEAD_DIM = 64
SSD_GROUPS = 2
SSD_STATE = 128
SSD_CONV = 4
SSD_CHUNK = 128
SSD_STEP_CHUNKS = 2
HEADS_PER_GROUP = SSD_HEADS // SSD_GROUPS
GROUP_DIM = HEADS_PER_GROUP * SSD_HEAD_DIM
GROUP_PAD = 256
SSD_INNER = SSD_HEADS * SSD_HEAD_DIM
SSD_PAD = SSD_GROUPS * GROUP_PAD
XBC_DIM = SSD_INNER + 2 * SSD_GROUPS * SSD_STATE
XBC_PAD = SSD_PAD + 2 * SSD_GROUPS * SSD_STATE

MLA_HEADS = 6
Q_LORA = 256
KV_LORA = 256
QK_NOPE = 64
QK_ROPE = 32
V_HEAD = 64
QK_DIM = QK_NOPE + QK_ROPE
MLA_OUT = MLA_HEADS * V_HEAD
ROPE_THETA = 10000.0
ROPE_PAIRS = QK_ROPE // 4
HEAD_SLAB = 128

POOL_WINDOWS = (2, 4, 8, 16)
POOL_GROUP_DIM = 64
POOL_DIM = len(POOL_WINDOWS) * POOL_GROUP_DIM
POOL_HALO = 8
CONV_HALO = 8

D_FF = 4 * D_MODEL
FF_CHUNK = 1024

OFF_Z = 0
OFF_XBC = OFF_Z + SSD_INNER
OFF_DT = OFF_XBC + XBC_DIM
OFF_QA = OFF_DT + 2 * SSD_HEADS
OFF_KVA = OFF_QA + Q_LORA
OFF_KROPE = OFF_KVA + KV_LORA
OFF_POOL = OFF_KROPE + QK_ROPE
IN_COLS = OFF_POOL + POOL_DIM

P_Z = 0
P_XBC = P_Z + SSD_PAD
P_QA = P_XBC + XBC_PAD
P_KVA = P_QA + Q_LORA
P_POOL = P_KVA + KV_LORA
P_DT = P_POOL + POOL_DIM
P_KR = P_DT + 128
P_COLS = P_KR + HEAD_SLAB

TM = 256
ATT_TQ = 256
ATT_TQ_LATENT = 512
ATT_TK = 512
BOUND_TK = 512
BOUND_SLACK = 1.0 + 2.0 ** -9
MAX_BOUND_GAP = 64.0
VT_ROWS = 80
NEG_BIG = -1e30
VMEM_LIMIT = 56 * 1024 * 1024
LOG2E = math.log2(math.e)


def _pad_group_index(base):
    idx = np.full((SSD_PAD,), -1, np.int64)
    for g in range(SSD_GROUPS):
        idx[g * GROUP_PAD:g * GROUP_PAD + GROUP_DIM] = base + g * GROUP_DIM + np.arange(GROUP_DIM)
    return idx


def _rope_perm():
    src = np.zeros((QK_ROPE,), np.int64)
    for half in range(2):
        for axis in range(2):
            for p in range(ROPE_PAIRS):
                src[half * 16 + axis * 8 + p] = axis * 16 + half * 8 + p
    return src


def _win_index():
    idx = np.full((P_COLS,), -1, np.int64)
    idx[P_Z:P_Z + SSD_PAD] = _pad_group_index(OFF_Z)
    idx[P_XBC:P_XBC + SSD_PAD] = _pad_group_index(OFF_XBC)
    nbc = 2 * SSD_GROUPS * SSD_STATE
    idx[P_XBC + SSD_PAD:P_XBC + SSD_PAD + nbc] = OFF_XBC + SSD_INNER + np.arange(nbc)
    idx[P_QA:P_QA + Q_LORA] = OFF_QA + np.arange(Q_LORA)
    idx[P_KVA:P_KVA + KV_LORA] = OFF_KVA + np.arange(KV_LORA)
    idx[P_POOL:P_POOL + POOL_DIM] = OFF_POOL + np.arange(POOL_DIM)
    idx[P_DT:P_DT + 2 * SSD_HEADS] = OFF_DT + np.arange(2 * SSD_HEADS)
    idx[P_KR + QK_NOPE:P_KR + QK_NOPE + QK_ROPE] = OFF_KROPE + _rope_perm()
    return idx


def _wq_index():
    idx = np.full((MLA_HEADS * HEAD_SLAB,), -1, np.int64)
    perm = _rope_perm()
    for h in range(MLA_HEADS):
        idx[h * HEAD_SLAB:h * HEAD_SLAB + QK_NOPE] = h * QK_DIM + np.arange(QK_NOPE)
        idx[h * HEAD_SLAB + QK_NOPE:h * HEAD_SLAB + QK_DIM] = h * QK_DIM + QK_NOPE + perm
    return idx


def _wkv_index():
    nk = MLA_HEADS * HEAD_SLAB
    idx = np.full((nk + MLA_OUT,), -1, np.int64)
    per_head = QK_NOPE + V_HEAD
    for h in range(MLA_HEADS):
        idx[h * HEAD_SLAB:h * HEAD_SLAB + QK_NOPE] = h * per_head + np.arange(QK_NOPE)
        idx[nk + h * V_HEAD:nk + (h + 1) * V_HEAD] = h * per_head + QK_NOPE + np.arange(V_HEAD)
    return idx


def _remap_cols(w, idx):
    take = jnp.take(w, jnp.asarray(np.maximum(idx, 0)), axis=-1)
    return jnp.where(jnp.asarray(idx >= 0), take, jnp.zeros((), w.dtype))


def _rope_tables(seq, ctx):
    t = np.arange(seq)
    row = jnp.asarray((t // GRID_W).astype(np.float32))
    col = jnp.asarray((t % GRID_W).astype(np.float32))
    inv_freq = ROPE_THETA ** (-jnp.arange(ROPE_PAIRS, dtype=F32) / ROPE_PAIRS)
    ang = jnp.concatenate([row[:, None] * inv_freq, col[:, None] * inv_freq], axis=1)
    cos16, sin16 = jnp.cos(ang), jnp.sin(ang)
    ones = jnp.ones((seq, QK_NOPE), F32)
    zeros = jnp.zeros((seq, QK_NOPE), F32)
    z16 = jnp.zeros((seq, 16), F32)
    tail1 = jnp.ones((seq, HEAD_SLAB - QK_DIM), F32)
    tail0 = jnp.zeros((seq, HEAD_SLAB - QK_DIM), F32)
    c_tab = jnp.concatenate([ones, cos16, cos16, tail1], axis=1)
    s1_tab = jnp.concatenate([zeros, -sin16, z16, tail0], axis=1)
    s2_tab = jnp.concatenate([zeros, z16, sin16, tail0], axis=1)
    c_tab = jnp.concatenate([c_tab, jnp.ones((ctx, HEAD_SLAB), F32)], axis=0)
    s1_tab = jnp.concatenate([s1_tab, jnp.zeros((ctx, HEAD_SLAB), F32)], axis=0)
    s2_tab = jnp.concatenate([s2_tab, jnp.zeros((ctx, HEAD_SLAB), F32)], axis=0)
    k_tabs = jnp.stack([c_tab, s1_tab, s2_tab])
    qscale = (QK_DIM ** -0.5) * LOG2E
    return jnp.concatenate([k_tabs, k_tabs * qscale], axis=0)


def _rms(u, w_row):
    ms = jnp.mean(u * u, axis=-1, keepdims=True)
    return u * lax.rsqrt(ms + EPS) * w_row


def _silu(u):
    return u * (1.0 / (1.0 + jnp.exp(-u)))


def _softplus(u):
    return jnp.maximum(u, 0.0) + jnp.log1p(jnp.exp(-jnp.abs(u)))


def _mod_kernel(cond_ref, w_ref, b_ref, o_ref):
    cnd = cond_ref[...]
    o_ref[0] = jnp.dot(_silu(cnd), w_ref[0], preferred_element_type=F32,
                       precision=lax.Precision.HIGHEST) + b_ref[0]


def _modulation(cond, mod_w, mod_b):
    depth = mod_w.shape[0]
    r = cond.shape[0]
    out = pl.pallas_call(
        _mod_kernel,
        grid=(depth, 6),
        in_specs=[pl.BlockSpec((r, D_MODEL), lambda d, j: (0, 0)),
                  pl.BlockSpec((1, D_MODEL, D_MODEL), lambda d, j: (d, 0, j)),
                  pl.BlockSpec((1, 1, D_MODEL), lambda d, j: (d, 0, j))],
        out_specs=pl.BlockSpec((1, r, D_MODEL), lambda d, j: (d, 0, j)),
        out_shape=jax.ShapeDtypeStruct((depth, r, 6 * D_MODEL), F32),
        name="adaln_modulation",
    )(cond, mod_w, mod_b.reshape(depth, 1, 6 * D_MODEL))
    return out.reshape(depth, r, 6, D_MODEL)


def _rope(t, c, s1, s2):
    return t * c + pltpu.roll(t, HEAD_SLAB - 16, 1) * s1 + pltpu.roll(t, 16, 1) * s2


def _token_tile(x_ref, ctx_ref, tile, n_lat_tiles):
    if ctx_ref is None:
        return x_ref[0]
    return jnp.where(tile >= n_lat_tiles, ctx_ref[0], x_ref[0])


def _inproj_kernel(*refs, n_lat_tiles, n_tiles, split_ctx):
    x_ref, ctx_ref = (refs[0], refs[1]) if split_ctx else (refs[0], None)
    (xprev_ref, xnext_ref, mod_ref, n1_ref, win_ref, convw_ref, convb_ref, qan_ref, kvan_ref, wq_ref, wkv_ref,
     rope_ref, z_ref, xs_ref, bc_ref, dt_ref, pool_ref, q_ref, k_ref, vt_ref, kn_ref,
     ext_scr) = refs[2 if split_ctx else 1:]

    def norm_mod(x):
        return (_rms(x, n1_ref[...]) * (1.0 + mod_ref[0, 1:2, :]) + mod_ref[0, 0:1, :]).astype(BF16)

    hb = norm_mod(_token_tile(x_ref, ctx_ref, pl.program_id(1), n_lat_tiles))

    def proj(lo, width):
        return jnp.dot(hb, win_ref[:, lo:lo + width], preferred_element_type=F32)

    z_ref[0] = proj(P_Z, SSD_PAD)
    pool_ref[0] = proj(P_POOL, POOL_DIM)

    tile = pl.program_id(1)
    seg_first = jnp.logical_or(tile == 0, tile == n_lat_tiles)
    seg_last = jnp.logical_or(tile == n_lat_tiles - 1, tile == n_tiles - 1)
    halo = norm_mod(jnp.concatenate([xprev_ref[0], xnext_ref[0]], axis=0))
    xbc = jnp.dot(jnp.concatenate([hb, halo], axis=0), win_ref[:, P_XBC:P_XBC + XBC_PAD],
                  preferred_element_type=F32)
    ext_scr[0:CONV_HALO, :] = xbc[TM:TM + CONV_HALO] * jnp.where(seg_first, 0.0, 1.0)
    ext_scr[CONV_HALO:CONV_HALO + TM, :] = xbc[0:TM]
    ext_scr[CONV_HALO + TM:, :] = xbc[TM + CONV_HALO:] * jnp.where(seg_last, 0.0, 1.0)
    u = convb_ref[...]
    for tap in range(SSD_CONV):
        lo = CONV_HALO - 1 + tap
        u = u + ext_scr[lo:lo + TM, :] * convw_ref[tap:tap + 1, :]
    u = _silu(u)
    xs_ref[0] = u[:, :SSD_PAD]
    bc_ref[0] = u[:, SSD_PAD:].astype(BF16)
    dt_ref[0] = proj(P_DT, 128)

    cq = _rms(proj(P_QA, Q_LORA), qan_ref[...]).astype(BF16)
    ckv = _rms(proj(P_KVA, KV_LORA), kvan_ref[...]).astype(BF16)
    q = jnp.dot(cq, wq_ref[...], preferred_element_type=F32)
    kv = jnp.dot(ckv, wkv_ref[...], preferred_element_type=F32)
    k_rope = _rope(proj(P_KR, HEAD_SLAB), rope_ref[0], rope_ref[1], rope_ref[2])
    knorm_rows = []
    for hd in range(MLA_HEADS):
        sl = slice(hd * HEAD_SLAB, (hd + 1) * HEAD_SLAB)
        q_ref[0, :, sl] = _rope(q[:, sl], rope_ref[3], rope_ref[4], rope_ref[5]).astype(BF16)
        k_b = (kv[:, sl] + k_rope).astype(BF16)
        k_ref[0, :, sl] = k_b
        k_f = k_b.astype(F32)
        n2 = jnp.max(jnp.sum(k_f * k_f, axis=-1, keepdims=True), axis=0, keepdims=True)
        knorm_rows.append(jnp.broadcast_to(n2, (1, HEAD_SLAB)))
    knorm_rows.append(jnp.zeros((8 - MLA_HEADS, HEAD_SLAB), F32))
    kn_ref[0, 0] = jnp.concatenate(knorm_rows, axis=0)
    nk = MLA_HEADS * HEAD_SLAB
    extra = VT_ROWS - V_HEAD
    ones_rows = (lax.broadcasted_iota(jnp.int32, (extra, TM), 0) == 0).astype(BF16)
    for pair in range(MLA_HEADS // 2):
        vt = kv[:, nk + pair * HEAD_SLAB:nk + (pair + 1) * HEAD_SLAB].T.astype(BF16)
        for sub in range(2):
            base = (2 * pair + sub) * VT_ROWS
            vt_ref[0, base:base + V_HEAD, :] = vt[sub * V_HEAD:(sub + 1) * V_HEAD, :]
            vt_ref[0, base + V_HEAD:base + VT_ROWS, :] = ones_rows


def _inproj(xc, ctx_in, mod, n1, win, convw, convb, qan, kvan, wq, wkv, rope, n_lat_tiles):
    bsz = xc.shape[0]
    split_ctx = ctx_in is not None
    t = xc.shape[1] + (ctx_in.shape[1] if split_ctx else 0)
    nt = t // TM
    ctx_row = mod.shape[0] - 1
    halo_blocks = xc.shape[1] // CONV_HALO
    last_x_tile = xc.shape[1] // TM - 1
    per = TM // CONV_HALO

    def tok(width):
        return pl.BlockSpec((1, TM, width), lambda b, i: (b, i, 0))

    x_specs = [pl.BlockSpec((1, TM, D_MODEL), lambda b, i: (b, jnp.minimum(i, last_x_tile), 0))]
    x_args = [xc]
    if split_ctx:
        x_specs.append(pl.BlockSpec((1, TM, D_MODEL), lambda b, i: (b, 0, 0)))
        x_args.append(ctx_in)

    def const(shape):
        return pl.BlockSpec(shape, lambda b, i: (0,) * len(shape))

    prev_spec = pl.BlockSpec((1, CONV_HALO, D_MODEL), lambda b, i: (b, jnp.maximum(i * per - 1, 0), 0))
    next_spec = pl.BlockSpec((1, CONV_HALO, D_MODEL),
                             lambda b, i: (b, jnp.minimum((i + 1) * per, halo_blocks - 1), 0))
    mod_spec = pl.BlockSpec((1, 6, D_MODEL), lambda b, i: (jnp.where(i >= n_lat_tiles, ctx_row, b), 0, 0))
    nh = MLA_HEADS * HEAD_SLAB
    nbc = 2 * SSD_GROUPS * SSD_STATE
    outs = [(SSD_PAD, F32), (SSD_PAD, F32), (nbc, BF16), (128, F32), (POOL_DIM, F32), (nh, BF16), (nh, BF16)]
    vt_rows = MLA_HEADS * VT_ROWS
    return pl.pallas_call(
        functools.partial(_inproj_kernel, n_lat_tiles=n_lat_tiles, n_tiles=nt, split_ctx=split_ctx),
        grid=(bsz, nt),
        in_specs=x_specs + [prev_spec, next_spec, mod_spec, const((1, D_MODEL)), const((D_MODEL, P_COLS)),
                  const(convw.shape), const(convb.shape),
                  const((1, Q_LORA)), const((1, KV_LORA)), const((Q_LORA, nh)), const((KV_LORA, nh + MLA_OUT)),
                  pl.BlockSpec((6, TM, HEAD_SLAB), lambda b, i: (0, i, 0))],
        out_specs=[tok(w) for w, _ in outs] + [pl.BlockSpec((1, vt_rows, TM), lambda b, i: (b, 0, i)),
                                                pl.BlockSpec((1, 1, 8, HEAD_SLAB), lambda b, i: (b, i, 0, 0))],
        out_shape=[jax.ShapeDtypeStruct((bsz, t, w), dt) for w, dt in outs]
                  + [jax.ShapeDtypeStruct((bsz, vt_rows, t), BF16),
                     jax.ShapeDtypeStruct((bsz, nt, 8, HEAD_SLAB), F32)],
        scratch_shapes=[pltpu.VMEM((TM + 2 * CONV_HALO, XBC_PAD), F32)],
        compiler_params=pltpu.CompilerParams(dimension_semantics=("parallel", "parallel"),
                                             vmem_limit_bytes=VMEM_LIMIT),
        name="inproj_mla",
    )(*x_args, xc, xc, mod, n1, win, convw, convb, qan, kvan, wq, wkv, rope)


def _expand_heads(cols, first_lane):
    n = cols.shape[0]
    lane = lax.broadcasted_iota(jnp.int32, (n, 128), 1)
    parts = []
    for g in range(SSD_GROUPS):
        b = [jnp.broadcast_to(cols[:, first_lane + HEADS_PER_GROUP * g + i:first_lane + HEADS_PER_GROUP * g + i + 1],
                              (n, 128)) for i in range(HEADS_PER_GROUP)]
        parts.append(jnp.where(lane < SSD_HEAD_DIM, b[0], b[1]))
        parts.append(b[2])
    return jnp.concatenate(parts, axis=1)


def _ssd_kernel(*refs, rev):
    h_scr = refs[-1]

    @pl.when(pl.program_id(1) == 0)
    def _():
        h_scr[...] = jnp.zeros_like(h_scr)

    order = range(SSD_STEP_CHUNKS - 1, -1, -1) if rev else range(SSD_STEP_CHUNKS)
    for sub in order:
        _ssd_one_chunk(refs, rev, slice(sub * SSD_CHUNK, (sub + 1) * SSD_CHUNK))


def _ssd_one_chunk(refs, rev, rows):
    if rev:
        xs_ref, bc_ref, dt_ref, dtb_ref, alog_ref, out_ref, h_scr = refs
    else:
        xs_ref, bc_ref, dt_ref, z_ref, yb_ref, dtb_ref, alog_ref, dskip_ref, nw_ref, out_ref, h_scr = refs
    L = SSD_CHUNK
    xs = xs_ref[0, rows, :]
    bmat = bc_ref[0, rows, 0:SSD_GROUPS * SSD_STATE]
    cmat = bc_ref[0, rows, SSD_GROUPS * SSD_STATE:]

    first_lane = SSD_HEADS if rev else 0
    dtv = _softplus(dt_ref[0, rows, :] + dtb_ref[...])
    adt = dtv * (-jnp.exp(alog_ref[...]))
    ri = lax.broadcasted_iota(jnp.int32, (L, L), 0)
    ci = lax.broadcasted_iota(jnp.int32, (L, L), 1)
    causal = (ci >= ri) if rev else (ci <= ri)
    cs_col = jnp.dot(causal.astype(F32), adt, preferred_element_type=F32,
                     precision=lax.Precision.HIGHEST)
    cs_row = cs_col.T

    dt_e = _expand_heads(dtv, first_lane)
    cs_e = _expand_heads(cs_col, first_lane)
    end_row = 0 if rev else L - 1
    cs_end = cs_e[end_row:end_row + 1, :]
    xdt = xs * dt_e
    xdt_b = xdt.astype(BF16)
    xdd_b = (xdt * jnp.exp(cs_end - cs_e)).astype(BF16)
    in_decay = jnp.exp(cs_e)
    state_decay = jnp.exp(cs_end)

    lane_g = lax.broadcasted_iota(jnp.int32, (1, GROUP_PAD), 1)
    ys = []
    for g in range(SSD_GROUPS):
        gs = slice(g * GROUP_PAD, (g + 1) * GROUP_PAD)
        ns = slice(g * SSD_STATE, (g + 1) * SSD_STATE)
        cm_g, bm_g = cmat[:, ns], bmat[:, ns]
        cb = lax.dot_general(cm_g, bm_g, (((1,), (1,)), ((), ())), preferred_element_type=F32)
        x_g = xdt_b[:, gs]
        y_g = jnp.zeros((L, GROUP_PAD), F32)
        for i in range(HEADS_PER_GROUP):
            hl = first_lane + HEADS_PER_GROUP * g + i
            diff = cs_col[:, hl:hl + 1] - cs_row[hl:hl + 1, :]
            decay = jnp.exp(jnp.where(causal, diff, NEG_BIG))
            head_lanes = jnp.logical_and(lane_g >= i * SSD_HEAD_DIM, lane_g < (i + 1) * SSD_HEAD_DIM)
            x_h = jnp.where(head_lanes, x_g, jnp.zeros((), BF16))
            y_g = y_g + jnp.dot((cb * decay).astype(BF16), x_h, preferred_element_type=F32)
        h_t = h_scr[g]
        y_in = jnp.dot(cm_g, h_t.astype(BF16), preferred_element_type=F32)
        y_g = y_g + y_in * in_decay[:, gs]
        s_new = lax.dot_general(bm_g, xdd_b[:, gs], (((0,), (0,)), ((), ())), preferred_element_type=F32)
        h_scr[g] = h_t * state_decay[:, gs] + s_new
        ys.append(y_g)
    y = jnp.concatenate(ys, axis=1)

    if rev:
        out_ref[0, rows, :] = y
    else:
        y = y + yb_ref[0, rows, :] + xs * dskip_ref[...]
        gated = y * _silu(z_ref[0, rows, :])
        outs = []
        for g in range(SSD_GROUPS):
            sl = gated[:, g * GROUP_PAD:(g + 1) * GROUP_PAD]
            ms = jnp.sum(sl * sl, axis=-1, keepdims=True) * (1.0 / GROUP_DIM)
            outs.append(sl * lax.rsqrt(ms + EPS))
        out_ref[0, rows, :] = (jnp.concatenate(outs, axis=1) * nw_ref[...]).astype(BF16)


def _ssd_tile(j, rev, n_lat_tiles, n_tiles):
    if rev:
        return n_tiles - 1 - j
    n_ctx = n_tiles - n_lat_tiles
    return jnp.where(j < n_ctx, n_lat_tiles + j, j - n_ctx)


def _ssd_scan(xs, bc, dt, z, yb, dtb, alog, dskip, nw, n_lat_chunks, rev):
    bsz, t, _ = xs.shape
    step_rows = SSD_STEP_CHUNKS * SSD_CHUNK
    n_chunks = t // step_rows
    chunk = functools.partial(_ssd_tile, rev=rev, n_lat_tiles=n_lat_chunks // SSD_STEP_CHUNKS, n_tiles=n_chunks)

    def tok(width):
        return pl.BlockSpec((1, step_rows, width), lambda b, j: (b, chunk(j), 0))

    def const(shape):
        return pl.BlockSpec(shape, lambda b, j: (0,) * len(shape))

    params = [dtb, alog]
    param_specs = [const(p.shape) for p in params]
    if rev:
        args = [xs, bc, dt] + params
        in_specs = [tok(SSD_PAD), tok(bc.shape[-1]), tok(128)] + param_specs
        out_dtype = F32
    else:
        args = [xs, bc, dt, z, yb] + params + [dskip, nw]
        in_specs = ([tok(SSD_PAD), tok(bc.shape[-1]), tok(128), tok(SSD_PAD), tok(SSD_PAD)] + param_specs
                    + [const(dskip.shape), const(nw.shape)])
        out_dtype = BF16
    return pl.pallas_call(
        functools.partial(_ssd_kernel, rev=rev),
        grid=(bsz, n_chunks),
        in_specs=in_specs,
        out_specs=tok(SSD_PAD),
        out_shape=jax.ShapeDtypeStruct((bsz, t, SSD_PAD), out_dtype),
        scratch_shapes=[pltpu.VMEM((SSD_GROUPS, SSD_STATE, GROUP_PAD), F32)],
        compiler_params=pltpu.CompilerParams(dimension_semantics=("parallel", "arbitrary"),
                                             vmem_limit_bytes=VMEM_LIMIT),
        name="ssd_bwd" if rev else "ssd_fwd",
    )(*args)


def _attn_kernel(*refs):
    q_ref, k_ref, vt_ref, kn_ref, o_ref, s_scr = refs
    tq = q_ref.shape[1]
    n_keys = k_ref.shape[1]
    chunks = [(lo, min(ATT_TK, n_keys - lo)) for lo in range(0, n_keys, ATT_TK)]
    mblk = max(8, min(ATT_TK, n_keys, (8 * 1024) // tq))

    def scores(hd, lo, size, m_acc):
        sl = slice(hd * HEAD_SLAB, (hd + 1) * HEAD_SLAB)
        s_c = lax.dot_general(k_ref[0, lo:lo + size, sl], q_ref[0, :, sl], (((1,), (1,)), ((), ())),
                              preferred_element_type=F32)
        s_scr[hd % 2, lo:lo + size, :] = s_c
        for r in range(0, size, mblk):
            blk = s_scr[hd % 2, lo + r:lo + r + mblk, :]
            m_acc = blk if m_acc is None else jnp.maximum(m_acc, blk)
        return m_acc

    def column_max(m_acc):
        rows = m_acc.shape[0]
        while rows > 8:
            rows //= 2
            m_acc = jnp.maximum(m_acc[0:rows], m_acc[rows:2 * rows])
        return jnp.max(m_acc, axis=0, keepdims=True)

    def finish(acc):
        return acc[0:V_HEAD] * (1.0 / acc[V_HEAD:V_HEAD + 1])

    def write_out(out_t):
        for pair in range(MLA_HEADS // 2):
            o_ref[0, :, pair * HEAD_SLAB:(pair + 1) * HEAD_SLAB] = (
                jnp.concatenate(out_t[2 * pair:2 * pair + 2], axis=0).T.astype(BF16))

    def bounded_path():
        k_max2 = jnp.max(kn_ref[0], axis=0)
        ones = jnp.ones((8, HEAD_SLAB), F32)
        small = [(lo, min(BOUND_TK, n_keys - lo)) for lo in range(0, n_keys, BOUND_TK)]
        slots = [(hd, lo, size) for hd in range(MLA_HEADS) for lo, size in small]

        def score_chunk(hd, lo, size):
            sl = slice(hd * HEAD_SLAB, (hd + 1) * HEAD_SLAB)
            return lax.dot_general(k_ref[0, lo:lo + size, sl], q_ref[0, :, sl], (((1,), (1,)), ((), ())),
                                   preferred_element_type=F32)

        bounds = []
        for hd in range(MLA_HEADS):
            q_f = q_ref[0, :, hd * HEAD_SLAB:(hd + 1) * HEAD_SLAB].astype(F32)
            q_n2 = lax.dot_general(ones, q_f * q_f, (((1,), (1,)), ((), ())), preferred_element_type=F32,
                                   precision=lax.Precision.HIGHEST)[0:1]
            bounds.append(jnp.sqrt(q_n2 * k_max2[hd:hd + 1, 0:1]) * BOUND_SLACK)

        out_t = []
        worst = None
        s_next = score_chunk(*slots[0])
        for idx, (hd, lo, size) in enumerate(slots):
            if lo == 0:
                bound = bounds[hd]
                m_acc = None
                acc = jnp.zeros((VT_ROWS, tq), F32)
            s_c = s_next
            if idx + 1 < len(slots):
                s_next = score_chunk(*slots[idx + 1])
            for r in range(0, size, mblk):
                blk = s_c[r:r + mblk]
                m_acc = blk if m_acc is None else jnp.maximum(m_acc, blk)
            acc = acc + jnp.dot(vt_ref[0, hd * VT_ROWS:(hd + 1) * VT_ROWS, lo:lo + size],
                                jnp.exp2(s_c - bound).astype(BF16), preferred_element_type=F32)
            if lo + size == n_keys:
                out_t.append(finish(acc))
                gap = bound - column_max(m_acc)
                worst = gap if worst is None else jnp.maximum(worst, gap)
        write_out(out_t)
        return jnp.max(worst)

    def exact_path():
        out_t = []
        m_prev = None
        for hd in range(MLA_HEADS + 1):
            m_acc = None
            acc = jnp.zeros((VT_ROWS, tq), F32)
            for lo, size in chunks:
                if hd < MLA_HEADS:
                    m_acc = scores(hd, lo, size, m_acc)
                if hd > 0:
                    p_t = jnp.exp2(s_scr[(hd - 1) % 2, lo:lo + size, :] - m_prev).astype(BF16)
                    acc = acc + jnp.dot(vt_ref[0, (hd - 1) * VT_ROWS:hd * VT_ROWS, lo:lo + size], p_t,
                                        preferred_element_type=F32)
            if hd > 0:
                out_t.append(finish(acc))
            if hd < MLA_HEADS:
                m_prev = column_max(m_acc)
        write_out(out_t)

    worst_gap = bounded_path()

    @pl.when(jnp.logical_not(worst_gap <= MAX_BOUND_GAP))
    def _():
        exact_path()


def _attention(q, k, vt, kn, seq, ctx, latent):
    bsz, t, _ = q.shape
    n_kn = kn.shape[1]
    assert ctx == TM
    tq = ATT_TQ_LATENT if latent else ATT_TQ
    n_tiles = (seq if latent else ctx) // tq
    q_off = 0 if latent else seq // tq
    n_keys = t if latent else ctx
    key_blk = 0 if latent else seq // ctx
    nh = MLA_HEADS * HEAD_SLAB
    in_specs = [pl.BlockSpec((1, tq, nh), lambda b, i: (b, q_off + i, 0)),
                pl.BlockSpec((1, n_keys, nh), lambda b, i: (b, key_blk, 0)),
                pl.BlockSpec((1, MLA_HEADS * VT_ROWS, n_keys), lambda b, i: (b, 0, key_blk)),
                pl.BlockSpec((1, n_kn, 8, HEAD_SLAB), lambda b, i: (b, 0, 0, 0)) if latent else
                pl.BlockSpec((1, 1, 8, HEAD_SLAB), lambda b, i: (b, n_kn - 1, 0, 0))]
    return pl.pallas_call(
        _attn_kernel,
        grid=(bsz, n_tiles),
        in_specs=in_specs,
        out_specs=pl.BlockSpec((1, tq, MLA_OUT), lambda b, i: (b, i, 0)),
        out_shape=jax.ShapeDtypeStruct((bsz, n_tiles * tq, MLA_OUT), BF16),
        scratch_shapes=[pltpu.VMEM((2, n_keys, tq), F32)],
        compiler_params=pltpu.CompilerParams(dimension_semantics=("parallel", "arbitrary"),
                                             vmem_limit_bytes=VMEM_LIMIT),
        name="mla_attention" if latent else "mla_attention_ctx",
    )(q, k, vt, kn)


def _pool_mix(pool_ref, pprev_ref, pnext_ref, ext_scr, pw_ref, ps_ref, tile, n_lat_tiles, n_tiles, seq, ctx):
    seg_first = jnp.logical_or(tile == 0, tile == n_lat_tiles)
    seg_last = jnp.logical_or(tile == n_lat_tiles - 1, tile == n_tiles - 1)
    h0 = POOL_HALO
    ext_scr[0:h0, :] = pprev_ref[0] * jnp.where(seg_first, 0.0, 1.0)
    ext_scr[h0:h0 + TM, :] = pool_ref[0]
    ext_scr[h0 + TM:h0 + TM + h0, :] = pnext_ref[0] * jnp.where(seg_last, 0.0, 1.0)

    is_ctx = tile >= n_lat_tiles
    seg_len = jnp.where(is_ctx, ctx, seq)
    pos = lax.broadcasted_iota(jnp.int32, (TM, 1), 0) + jnp.where(is_ctx, tile - n_lat_tiles, tile) * TM
    lane = lax.broadcasted_iota(jnp.int32, (1, 128), 1)
    low_half = lane < POOL_GROUP_DIM

    def inv_count(w):
        cnt = jnp.minimum(pos + (w - w // 2), seg_len) - jnp.maximum(pos - w // 2, 0)
        return 1.0 / cnt.astype(F32)

    def taps(slab, offsets):
        tot = None
        for k in offsets:
            piece = ext_scr[h0 + k:h0 + k + TM, slab * 128:(slab + 1) * 128]
            tot = piece if tot is None else tot + piece
        return tot

    outs = []
    for slab in range(2):
        w_small, w_big = POOL_WINDOWS[2 * slab], POOL_WINDOWS[2 * slab + 1]
        small = taps(slab, range(-(w_small // 2), w_small - w_small // 2))
        extra = [k for k in range(-(w_big // 2), w_big - w_big // 2)
                 if not -(w_small // 2) <= k < w_small - w_small // 2]
        big = small + taps(slab, extra)
        mean = jnp.where(low_half, small * inv_count(w_small), big * inv_count(w_big))
        outs.append(mean - ext_scr[h0:h0 + TM, slab * 128:(slab + 1) * 128])
    d = jnp.concatenate(outs, axis=1).astype(BF16)
    return jnp.dot(d, pw_ref[...], preferred_element_type=F32) * ps_ref[...]


def _mix_mlp_kernel(*refs, final, n_lat_tiles, n_tiles, seq, ctx, split_ctx, split_attn):
    x_ref, ctx_ref = (refs[0], refs[1]) if split_ctx else (refs[0], None)
    refs = refs[2 if split_ctx else 1:]
    ssd_ref, attn_ref = refs[0], refs[1]
    attn_ctx_ref = refs[2] if split_attn else None
    refs = refs[3 if split_attn else 2:]
    if final:
        (pool_ref, pprev_ref, pnext_ref, mod_ref, wos_ref, woa_ref, wop_ref,
         pw_ref, ps_ref, n2_ref, w1_ref, w2_ref, fn_ref, o_ref, ext_scr) = refs
    else:
        (pool_ref, pprev_ref, pnext_ref, mod_ref, wos_ref, woa_ref, wop_ref,
         pw_ref, ps_ref, n2_ref, w1_ref, w2_ref, o_ref, ext_scr) = refs
    tile = pl.program_id(1)
    pool_y = _pool_mix(pool_ref, pprev_ref, pnext_ref, ext_scr, pw_ref, ps_ref, tile,
                       n_lat_tiles, n_tiles, seq, ctx)
    mix = (jnp.dot(ssd_ref[0], wos_ref[...], preferred_element_type=F32)
           + jnp.dot(_token_tile(attn_ref, attn_ctx_ref, tile, n_lat_tiles), woa_ref[...],
                     preferred_element_type=F32)
           + jnp.dot(pool_y.astype(BF16), wop_ref[...], preferred_element_type=F32))
    x1 = _token_tile(x_ref, ctx_ref, tile, n_lat_tiles) + mod_ref[0, 2:3, :] * mix
    h = _rms(x1, n2_ref[...])
    hb = (h * (1.0 + mod_ref[0, 4:5, :]) + mod_ref[0, 3:4, :]).astype(BF16)
    acc = jnp.zeros((TM, D_MODEL), F32)
    for lo in range(0, D_FF, FF_CHUNK):
        a = jnp.maximum(jnp.dot(hb, w1_ref[:, lo:lo + FF_CHUNK], preferred_element_type=F32), 0.0)
        acc = acc + jnp.dot((a * a).astype(BF16), w2_ref[lo:lo + FF_CHUNK, :], preferred_element_type=F32)
    x2 = x1 + mod_ref[0, 5:6, :] * acc
    if final:
        x2 = _rms(x2, fn_ref[...])
    o_ref[0] = x2


def _mix_mlp(xc, ctx_in, ssd, attn, attn_ctx, pool, mod, wos, woa, wop, pw, ps, n2, w1, w2, fn, n_lat_tiles,
             seq, ctx, final):
    bsz, t, _ = ssd.shape
    split_ctx = ctx_in is not None
    split_attn = attn_ctx is not None
    assert final or split_attn
    n_tiles = t // TM
    out_tiles = n_lat_tiles if final else n_tiles
    halo_blocks = t // POOL_HALO
    per = TM // POOL_HALO
    ctx_row = mod.shape[0] - 1
    last_x_tile = xc.shape[1] // TM - 1

    def tok(width):
        return pl.BlockSpec((1, TM, width), lambda b, i: (b, i, 0))

    x_specs = [pl.BlockSpec((1, TM, D_MODEL), lambda b, i: (b, jnp.minimum(i, last_x_tile), 0))]
    x_args = [xc]
    if split_ctx:
        x_specs.append(pl.BlockSpec((1, TM, D_MODEL), lambda b, i: (b, 0, 0)))
        x_args.append(ctx_in)

    attn_specs = [pl.BlockSpec((1, TM, MLA_OUT), lambda b, i: (b, jnp.minimum(i, n_lat_tiles - 1), 0))]
    attn_args = [attn]
    if split_attn:
        attn_specs.append(pl.BlockSpec((1, TM, MLA_OUT), lambda b, i: (b, 0, 0)))
        attn_args.append(attn_ctx)

    def const(shape):
        return pl.BlockSpec(shape, lambda b, i: (0,) * len(shape), pipeline_mode=pl.Buffered(1))

    prev_spec = pl.BlockSpec((1, POOL_HALO, POOL_DIM), lambda b, i: (b, jnp.maximum(i * per - 1, 0), 0))
    next_spec = pl.BlockSpec((1, POOL_HALO, POOL_DIM),
                             lambda b, i: (b, jnp.minimum((i + 1) * per, halo_blocks - 1), 0))
    mod_spec = pl.BlockSpec((1, 6, D_MODEL), lambda b, i: (jnp.where(i >= n_lat_tiles, ctx_row, b), 0, 0))
    weights = [wos, woa, wop, pw, ps, n2, w1, w2] + ([fn] if final else [])
    return pl.pallas_call(
        functools.partial(_mix_mlp_kernel, final=final, n_lat_tiles=n_lat_tiles, n_tiles=n_tiles,
                          seq=seq, ctx=ctx, split_ctx=split_ctx, split_attn=split_attn),
        grid=(bsz, out_tiles),
        in_specs=x_specs + [tok(SSD_PAD)] + attn_specs + [tok(POOL_DIM), prev_spec, next_spec, mod_spec]
                 + [const(w.shape) for w in weights],
        out_specs=tok(D_MODEL),
        out_shape=jax.ShapeDtypeStruct((bsz, out_tiles * TM, D_MODEL), F32),
        scratch_shapes=[pltpu.VMEM((TM + 2 * POOL_HALO, POOL_DIM), F32)],
        compiler_params=pltpu.CompilerParams(dimension_semantics=("parallel", "parallel"),
                                             vmem_limit_bytes=VMEM_LIMIT),
        name="mix_mlp_final" if final else "mix_mlp",
    )(*x_args, ssd, *attn_args, pool, pool, pool, mod, *weights)


def kernel(x, c, ctx, c_ctx, mod_w, mod_b, norm1_w, norm2_w, w_in, conv_w, conv_b, dt_bias, a_log, ssd_d,
           ssd_norm_w, q_a_norm_w, w_q_b, kv_a_norm_w, w_kv_b, pool_w, pool_scale, w_out, w_mlp1, w_mlp2,
           final_norm_w):
    bsz, seq, _ = x.shape
    n_ctx = ctx.shape[1]
    depth = mod_w.shape[0]
    assert seq % TM == 0 and n_ctx % TM == 0 and seq % GRID_W == 0
    assert seq % (SSD_STEP_CHUNKS * SSD_CHUNK) == 0 and n_ctx % (SSD_STEP_CHUNKS * SSD_CHUNK) == 0
    assert seq % ATT_TQ_LATENT == 0 and n_ctx % ATT_TQ == 0 and seq % n_ctx == 0
    n_lat_tiles = seq // TM
    n_lat_chunks = seq // SSD_CHUNK

    assert n_ctx == TM
    xc, ctx_in = x, ctx
    cond_rows = -(-(bsz + 1) // 8) * 8
    cond = jnp.zeros((cond_rows, D_MODEL), F32).at[:bsz].set(c).at[bsz].set(c_ctx)
    mod_all = _modulation(cond, mod_w, mod_b)[:, :bsz + 1]
    rope = _rope_tables(seq, n_ctx)

    win_idx, wq_idx, wkv_idx = _win_index(), _wq_index(), _wkv_index()
    pad_idx = _pad_group_index(0)
    xbc_idx = np.concatenate([pad_idx, SSD_INNER + np.arange(XBC_DIM - SSD_INNER)])
    row = lambda v: v.reshape(1, -1)

    for i in range(depth):
        final = i == depth - 1
        win = _remap_cols(w_in[i], win_idx).astype(BF16)
        wq = _remap_cols(w_q_b[i], wq_idx).astype(BF16)
        wkv = _remap_cols(w_kv_b[i], wkv_idx).astype(BF16)
        convw = _remap_cols(conv_w[i], xbc_idx)
        convb = row(_remap_cols(conv_b[i], xbc_idx))
        dtb = row(jnp.pad(dt_bias[i].reshape(-1), (0, 128 - 2 * SSD_HEADS)))
        alog = row(jnp.pad(a_log[i].reshape(-1), (0, 128 - 2 * SSD_HEADS)))
        dskip = row(_remap_cols(jnp.repeat(ssd_d[i], SSD_HEAD_DIM), pad_idx))
        nw = row(_remap_cols(ssd_norm_w[i], pad_idx))
        wo = w_out[i].astype(BF16)
        wos = _remap_cols(wo[:SSD_INNER].T, pad_idx).T
        woa = wo[SSD_INNER:SSD_INNER + MLA_OUT]
        wop = wo[SSD_INNER + MLA_OUT:]
        pw = jax.scipy.linalg.block_diag(*[pool_w[i, g] for g in range(len(POOL_WINDOWS))]).astype(BF16)

        z, xs, bc, dt, pool, q, k, v, kn = _inproj(xc, ctx_in, mod_all[i], row(norm1_w[i]), win, convw, convb,
                                                    row(q_a_norm_w[i]), row(kv_a_norm_w[i]), wq, wkv, rope,
                                                    n_lat_tiles)
        yb = _ssd_scan(xs, bc, dt, None, None, dtb, alog, None, None, n_lat_chunks, rev=True)
        ssd = _ssd_scan(xs, bc, dt, z, yb, dtb, alog, dskip, nw, n_lat_chunks, rev=False)
        attn = _attention(q, k, v, kn, seq, n_ctx, latent=True)
        attn_ctx = None if final else _attention(q, k, v, kn, seq, n_ctx, latent=False)
        xc = _mix_mlp(xc, ctx_in, ssd, attn, attn_ctx, pool, mod_all[i], wos, woa, wop, pw, row(pool_scale[i]),
                      row(norm2_w[i]), w_mlp1[i].astype(BF16), w_mlp2[i].astype(BF16), row(final_norm_w),
                      n_lat_tiles, seq, n_ctx, final)
        ctx_in = None
    return xc
```

```python
import functools
import math

import jax
import jax.numpy as jnp
import numpy as np
from jax import lax
from jax.experimental import pallas as pl
from jax.experimental.pallas import tpu as pltpu

F32 = jnp.float32
BF16 = jnp.bfloat16

D_MODEL = 1024
GRID_W = 64
EPS = 1e-6

SSD_HEADS = 6
SSD_HEAD_DIM = 64
SSD_GROUPS = 2
SSD_STATE = 128
SSD_CONV = 4
SSD_CHUNK = 128
SSD_STEP_CHUNKS = 2
HEADS_PER_GROUP = SSD_HEADS // SSD_GROUPS
GROUP_DIM = HEADS_PER_GROUP * SSD_HEAD_DIM
GROUP_PAD = 256
SSD_INNER = SSD_HEADS * SSD_HEAD_DIM
SSD_PAD = SSD_GROUPS * GROUP_PAD
XBC_DIM = SSD_INNER + 2 * SSD_GROUPS * SSD_STATE
XBC_PAD = SSD_PAD + 2 * SSD_GROUPS * SSD_STATE

MLA_HEADS = 6
Q_LORA = 256
KV_LORA = 256
QK_NOPE = 64
QK_ROPE = 32
V_HEAD = 64
QK_DIM = QK_NOPE + QK_ROPE
MLA_OUT = MLA_HEADS * V_HEAD
ROPE_THETA = 10000.0
ROPE_PAIRS = QK_ROPE // 4
HEAD_SLAB = 128

POOL_WINDOWS = (2, 4, 8, 16)
POOL_GROUP_DIM = 64
POOL_DIM = len(POOL_WINDOWS) * POOL_GROUP_DIM
POOL_HALO = 8
CONV_HALO = 8

D_FF = 4 * D_MODEL
FF_CHUNK = 1024

OFF_Z = 0
OFF_XBC = OFF_Z + SSD_INNER
OFF_DT = OFF_XBC + XBC_DIM
OFF_QA = OFF_DT + 2 * SSD_HEADS
OFF_KVA = OFF_QA + Q_LORA
OFF_KROPE = OFF_KVA + KV_LORA
OFF_POOL = OFF_KROPE + QK_ROPE
IN_COLS = OFF_POOL + POOL_DIM

P_Z = 0
P_XBC = P_Z + SSD_PAD
P_QA = P_XBC + XBC_PAD
P_KVA = P_QA + Q_LORA
P_POOL = P_KVA + KV_LORA
P_DT = P_POOL + POOL_DIM
P_KR = P_DT + 128
P_COLS = P_KR + HEAD_SLAB

TM = 256
ATT_TQ = 256
ATT_TQ_LATENT = 512
ATT_TK = 512
BOUND_TK = 512
BOUND_SLACK = 1.0 + 2.0 ** -7
MAX_BOUND_GAP = 64.0
VT_ROWS = 80
NEG_BIG = -1e30
VMEM_LIMIT = 56 * 1024 * 1024
LOG2E = math.log2(math.e)


def _pad_group_index(base):
    idx = np.full((SSD_PAD,), -1, np.int64)
    for g in range(SSD_GROUPS):
        idx[g * GROUP_PAD:g * GROUP_PAD + GROUP_DIM] = base + g * GROUP_DIM + np.arange(GROUP_DIM)
    return idx


def _rope_perm():
    src = np.zeros((QK_ROPE,), np.int64)
    for half in range(2):
        for axis in range(2):
            for p in range(ROPE_PAIRS):
                src[half * 16 + axis * 8 + p] = axis * 16 + half * 8 + p
    return src


def _win_index():
    idx = np.full((P_COLS,), -1, np.int64)
    idx[P_Z:P_Z + SSD_PAD] = _pad_group_index(OFF_Z)
    idx[P_XBC:P_XBC + SSD_PAD] = _pad_group_index(OFF_XBC)
    nbc = 2 * SSD_GROUPS * SSD_STATE
    idx[P_XBC + SSD_PAD:P_XBC + SSD_PAD + nbc] = OFF_XBC + SSD_INNER + np.arange(nbc)
    idx[P_QA:P_QA + Q_LORA] = OFF_QA + np.arange(Q_LORA)
    idx[P_KVA:P_KVA + KV_LORA] = OFF_KVA + np.arange(KV_LORA)
    idx[P_POOL:P_POOL + POOL_DIM] = OFF_POOL + np.arange(POOL_DIM)
    idx[P_DT:P_DT + 2 * SSD_HEADS] = OFF_DT + np.arange(2 * SSD_HEADS)
    idx[P_KR + QK_NOPE:P_KR + QK_NOPE + QK_ROPE] = OFF_KROPE + _rope_perm()
    return idx


def _wq_index():
    idx = np.full((MLA_HEADS * HEAD_SLAB,), -1, np.int64)
    perm = _rope_perm()
    for h in range(MLA_HEADS):
        idx[h * HEAD_SLAB:h * HEAD_SLAB + QK_NOPE] = h * QK_DIM + np.arange(QK_NOPE)
        idx[h * HEAD_SLAB + QK_NOPE:h * HEAD_SLAB + QK_DIM] = h * QK_DIM + QK_NOPE + perm
    return idx


def _wkv_index():
    nk = MLA_HEADS * HEAD_SLAB
    idx = np.full((nk + MLA_OUT,), -1, np.int64)
    per_head = QK_NOPE + V_HEAD
    for h in range(MLA_HEADS):
        idx[h * HEAD_SLAB:h * HEAD_SLAB + QK_NOPE] = h * per_head + np.arange(QK_NOPE)
        idx[nk + h * V_HEAD:nk + (h + 1) * V_HEAD] = h * per_head + QK_NOPE + np.arange(V_HEAD)
    return idx


def _remap_cols(w, idx):
    take = jnp.take(w, jnp.asarray(np.maximum(idx, 0)), axis=-1)
    return jnp.where(jnp.asarray(idx >= 0), take, jnp.zeros((), w.dtype))


def _rope_tables(seq, ctx):
    t = np.arange(seq)
    row = jnp.asarray((t // GRID_W).astype(np.float32))
    col = jnp.asarray((t % GRID_W).astype(np.float32))
    inv_freq = ROPE_THETA ** (-jnp.arange(ROPE_PAIRS, dtype=F32) / ROPE_PAIRS)
    ang = jnp.concatenate([row[:, None] * inv_freq, col[:, None] * inv_freq], axis=1)
    cos16, sin16 = jnp.cos(ang), jnp.sin(ang)
    ones = jnp.ones((seq, QK_NOPE), F32)
    zeros = jnp.zeros((seq, QK_NOPE), F32)
    z16 = jnp.zeros((seq, 16), F32)
    tail1 = jnp.ones((seq, HEAD_SLAB - QK_DIM), F32)
    tail0 = jnp.zeros((seq, HEAD_SLAB - QK_DIM), F32)
    c_tab = jnp.concatenate([ones, cos16, cos16, tail1], axis=1)
    s1_tab = jnp.concatenate([zeros, -sin16, z16, tail0], axis=1)
    s2_tab = jnp.concatenate([zeros, z16, sin16, tail0], axis=1)
    c_tab = jnp.concatenate([c_tab, jnp.ones((ctx, HEAD_SLAB), F32)], axis=0)
    s1_tab = jnp.concatenate([s1_tab, jnp.zeros((ctx, HEAD_SLAB), F32)], axis=0)
    s2_tab = jnp.concatenate([s2_tab, jnp.zeros((ctx, HEAD_SLAB), F32)], axis=0)
    k_tabs = jnp.stack([c_tab, s1_tab, s2_tab])
    qscale = (QK_DIM ** -0.5) * LOG2E
    return jnp.concatenate([k_tabs, k_tabs * qscale], axis=0)


def _rms(u, w_row):
    ms = jnp.mean(u * u, axis=-1, keepdims=True)
    return u * lax.rsqrt(ms + EPS) * w_row


def _silu(u):
    return u * (1.0 / (1.0 + jnp.exp(-u)))


def _softplus(u):
    return jnp.maximum(u, 0.0) + jnp.log1p(jnp.exp(-jnp.abs(u)))


def _mod_kernel(cond_ref, w_ref, b_ref, o_ref):
    cnd = cond_ref[...]
    o_ref[0] = jnp.dot(_silu(cnd), w_ref[0], preferred_element_type=F32,
                       precision=lax.Precision.HIGHEST) + b_ref[0]


def _modulation(cond, mod_w, mod_b):
    depth = mod_w.shape[0]
    r = cond.shape[0]
    out = pl.pallas_call(
        _mod_kernel,
        grid=(depth, 6),
        in_specs=[pl.BlockSpec((r, D_MODEL), lambda d, j: (0, 0)),
                  pl.BlockSpec((1, D_MODEL, D_MODEL), lambda d, j: (d, 0, j)),
                  pl.BlockSpec((1, 1, D_MODEL), lambda d, j: (d, 0, j))],
        out_specs=pl.BlockSpec((1, r, D_MODEL), lambda d, j: (d, 0, j)),
        out_shape=jax.ShapeDtypeStruct((depth, r, 6 * D_MODEL), F32),
        name="adaln_modulation",
    )(cond, mod_w, mod_b.reshape(depth, 1, 6 * D_MODEL))
    return out.reshape(depth, r, 6, D_MODEL)


def _rope(t, c, s1, s2):
    return t * c + pltpu.roll(t, HEAD_SLAB - 16, 1) * s1 + pltpu.roll(t, 16, 1) * s2


def _token_tile(x_ref, ctx_ref, tile, n_lat_tiles):
    if ctx_ref is None:
        return x_ref[0]
    return jnp.where(tile >= n_lat_tiles, ctx_ref[0], x_ref[0])


def _inproj_kernel(*refs, n_lat_tiles, n_tiles, split_ctx):
    x_ref, ctx_ref = (refs[0], refs[1]) if split_ctx else (refs[0], None)
    (xprev_ref, xnext_ref, mod_ref, n1_ref, win_ref, convw_ref, convb_ref, qan_ref, kvan_ref, wq_ref, wkv_ref,
     rope_ref, z_ref, xs_ref, bc_ref, dt_ref, pool_ref, q_ref, k_ref, vt_ref, kn_ref,
     ext_scr) = refs[2 if split_ctx else 1:]

    def norm_mod(x):
        return (_rms(x, n1_ref[...]) * (1.0 + mod_ref[0, 1:2, :]) + mod_ref[0, 0:1, :]).astype(BF16)

    hb = norm_mod(_token_tile(x_ref, ctx_ref, pl.program_id(1), n_lat_tiles))

    def proj(lo, width):
        return jnp.dot(hb, win_ref[:, lo:lo + width], preferred_element_type=F32)

    z_ref[0] = proj(P_Z, SSD_PAD)
    pool_ref[0] = proj(P_POOL, POOL_DIM)

    tile = pl.program_id(1)
    seg_first = jnp.logical_or(tile == 0, tile == n_lat_tiles)
    seg_last = jnp.logical_or(tile == n_lat_tiles - 1, tile == n_tiles - 1)
    halo = norm_mod(jnp.concatenate([xprev_ref[0], xnext_ref[0]], axis=0))
    xbc = jnp.dot(jnp.concatenate([hb, halo], axis=0), win_ref[:, P_XBC:P_XBC + XBC_PAD],
                  preferred_element_type=F32)
    ext_scr[0:CONV_HALO, :] = xbc[TM:TM + CONV_HALO] * jnp.where(seg_first, 0.0, 1.0)
    ext_scr[CONV_HALO:CONV_HALO + TM, :] = xbc[0:TM]
    ext_scr[CONV_HALO + TM:, :] = xbc[TM + CONV_HALO:] * jnp.where(seg_last, 0.0, 1.0)
    u = convb_ref[...]
    for tap in range(SSD_CONV):
        lo = CONV_HALO - 1 + tap
        u = u + ext_scr[lo:lo + TM, :] * convw_ref[tap:tap + 1, :]
    u = _silu(u)
    xs_ref[0] = u[:, :SSD_PAD]
    bc_ref[0] = u[:, SSD_PAD:].astype(BF16)
    dt_ref[0] = proj(P_DT, 128)

    cq = _rms(proj(P_QA, Q_LORA), qan_ref[...]).astype(BF16)
    ckv = _rms(proj(P_KVA, KV_LORA), kvan_ref[...]).astype(BF16)
    q = jnp.dot(cq, wq_ref[...], preferred_element_type=F32)
    kv = jnp.dot(ckv, wkv_ref[...], preferred_element_type=F32)
    k_rope = _rope(proj(P_KR, HEAD_SLAB), rope_ref[0], rope_ref[1], rope_ref[2])
    knorm_rows = []
    for hd in range(MLA_HEADS):
        sl = slice(hd * HEAD_SLAB, (hd + 1) * HEAD_SLAB)
        q_ref[0, :, sl] = _rope(q[:, sl], rope_ref[3], rope_ref[4], rope_ref[5]).astype(BF16)
        k_b = (kv[:, sl] + k_rope).astype(BF16)
        k_ref[0, :, sl] = k_b
        k_f = k_b.astype(F32)
        n2 = jnp.dot((k_f * k_f).astype(BF16), jnp.ones((HEAD_SLAB, HEAD_SLAB), BF16), preferred_element_type=F32)
        knorm_rows.append(jnp.max(n2, axis=0, keepdims=True))
    knorm_rows.append(jnp.zeros((8 - MLA_HEADS, HEAD_SLAB), F32))
    kn_ref[0, 0] = jnp.concatenate(knorm_rows, axis=0)
    nk = MLA_HEADS * HEAD_SLAB
    extra = VT_ROWS - V_HEAD
    ones_rows = (lax.broadcasted_iota(jnp.int32, (extra, TM), 0) == 0).astype(BF16)
    for pair in range(MLA_HEADS // 2):
        vt = kv[:, nk + pair * HEAD_SLAB:nk + (pair + 1) * HEAD_SLAB].T.astype(BF16)
        for sub in range(2):
            base = (2 * pair + sub) * VT_ROWS
            vt_ref[0, base:base + V_HEAD, :] = vt[sub * V_HEAD:(sub + 1) * V_HEAD, :]
            vt_ref[0, base + V_HEAD:base + VT_ROWS, :] = ones_rows


def _inproj(xc, ctx_in, mod, n1, win, convw, convb, qan, kvan, wq, wkv, rope, n_lat_tiles):
    bsz = xc.shape[0]
    split_ctx = ctx_in is not None
    t = xc.shape[1] + (ctx_in.shape[1] if split_ctx else 0)
    nt = t // TM
    ctx_row = mod.shape[0] - 1
    halo_blocks = xc.shape[1] // CONV_HALO
    last_x_tile = xc.shape[1] // TM - 1
    per = TM // CONV_HALO

    def tok(width):
        return pl.BlockSpec((1, TM, width), lambda b, i: (b, i, 0))

    x_specs = [pl.BlockSpec((1, TM, D_MODEL), lambda b, i: (b, jnp.minimum(i, last_x_tile), 0))]
    x_args = [xc]
    if split_ctx:
        x_specs.append(pl.BlockSpec((1, TM, D_MODEL), lambda b, i: (b, 0, 0)))
        x_args.append(ctx_in)

    def const(shape):
        return pl.BlockSpec(shape, lambda b, i: (0,) * len(shape))

    prev_spec = pl.BlockSpec((1, CONV_HALO, D_MODEL), lambda b, i: (b, jnp.maximum(i * per - 1, 0), 0))
    next_spec = pl.BlockSpec((1, CONV_HALO, D_MODEL),
                             lambda b, i: (b, jnp.minimum((i + 1) * per, halo_blocks - 1), 0))
    mod_spec = pl.BlockSpec((1, 6, D_MODEL), lambda b, i: (jnp.where(i >= n_lat_tiles, ctx_row, b), 0, 0))
    nh = MLA_HEADS * HEAD_SLAB
    nbc = 2 * SSD_GROUPS * SSD_STATE
    outs = [(SSD_PAD, F32), (SSD_PAD, F32), (nbc, BF16), (128, F32), (POOL_DIM, F32), (nh, BF16), (nh, BF16)]
    vt_rows = MLA_HEADS * VT_ROWS
    return pl.pallas_call(
        functools.partial(_inproj_kernel, n_lat_tiles=n_lat_tiles, n_tiles=nt, split_ctx=split_ctx),
        grid=(bsz, nt),
        in_specs=x_specs + [prev_spec, next_spec, mod_spec, const((1, D_MODEL)), const((D_MODEL, P_COLS)),
                  const(convw.shape), const(convb.shape),
                  const((1, Q_LORA)), const((1, KV_LORA)), const((Q_LORA, nh)), const((KV_LORA, nh + MLA_OUT)),
                  pl.BlockSpec((6, TM, HEAD_SLAB), lambda b, i: (0, i, 0))],
        out_specs=[tok(w) for w, _ in outs] + [pl.BlockSpec((1, vt_rows, TM), lambda b, i: (b, 0, i)),
                                                pl.BlockSpec((1, 1, 8, HEAD_SLAB), lambda b, i: (b, i, 0, 0))],
        out_shape=[jax.ShapeDtypeStruct((bsz, t, w), dt) for w, dt in outs]
                  + [jax.ShapeDtypeStruct((bsz, vt_rows, t), BF16),
                     jax.ShapeDtypeStruct((bsz, nt, 8, HEAD_SLAB), F32)],
        scratch_shapes=[pltpu.VMEM((TM + 2 * CONV_HALO, XBC_PAD), F32)],
        compiler_params=pltpu.CompilerParams(dimension_semantics=("parallel", "parallel"),
                                             vmem_limit_bytes=VMEM_LIMIT),
        name="inproj_mla",
    )(*x_args, xc, xc, mod, n1, win, convw, convb, qan, kvan, wq, wkv, rope)


def _expand_heads(cols, first_lane):
    n = cols.shape[0]
    lane = lax.broadcasted_iota(jnp.int32, (n, 128), 1)
    parts = []
    for g in range(SSD_GROUPS):
        b = [jnp.broadcast_to(cols[:, first_lane + HEADS_PER_GROUP * g + i:first_lane + HEADS_PER_GROUP * g + i + 1],
                              (n, 128)) for i in range(HEADS_PER_GROUP)]
        parts.append(jnp.where(lane < SSD_HEAD_DIM, b[0], b[1]))
        parts.append(b[2])
    return jnp.concatenate(parts, axis=1)


def _ssd_kernel(*refs, rev):
    h_scr = refs[-1]

    @pl.when(pl.program_id(1) == 0)
    def _():
        h_scr[...] = jnp.zeros_like(h_scr)

    order = range(SSD_STEP_CHUNKS - 1, -1, -1) if rev else range(SSD_STEP_CHUNKS)
    for sub in order:
        _ssd_one_chunk(refs, rev, slice(sub * SSD_CHUNK, (sub + 1) * SSD_CHUNK))


def _ssd_one_chunk(refs, rev, rows):
    if rev:
        xs_ref, bc_ref, dt_ref, dtb_ref, alog_ref, out_ref, h_scr = refs
    else:
        xs_ref, bc_ref, dt_ref, z_ref, yb_ref, dtb_ref, alog_ref, dskip_ref, nw_ref, out_ref, h_scr = refs
    L = SSD_CHUNK
    xs = xs_ref[0, rows, :]
    bmat = bc_ref[0, rows, 0:SSD_GROUPS * SSD_STATE]
    cmat = bc_ref[0, rows, SSD_GROUPS * SSD_STATE:]

    first_lane = SSD_HEADS if rev else 0
    dtv = _softplus(dt_ref[0, rows, :] + dtb_ref[...])
    adt = dtv * (-jnp.exp(alog_ref[...]))
    ri = lax.broadcasted_iota(jnp.int32, (L, L), 0)
    ci = lax.broadcasted_iota(jnp.int32, (L, L), 1)
    causal = (ci >= ri) if rev else (ci <= ri)
    cs_col = jnp.dot(causal.astype(F32), adt, preferred_element_type=F32,
                     precision=lax.Precision.HIGHEST)
    cs_row = cs_col.T

    dt_e = _expand_heads(dtv, first_lane)
    cs_e = _expand_heads(cs_col, first_lane)
    end_row = 0 if rev else L - 1
    cs_end = cs_e[end_row:end_row + 1, :]
    xdt = xs * dt_e
    xdt_b = xdt.astype(BF16)
    xdd_b = (xdt * jnp.exp(cs_end - cs_e)).astype(BF16)
    in_decay = jnp.exp(cs_e)
    state_decay = jnp.exp(cs_end)

    lane_g = lax.broadcasted_iota(jnp.int32, (1, GROUP_PAD), 1)
    ys = []
    for g in range(SSD_GROUPS):
        gs = slice(g * GROUP_PAD, (g + 1) * GROUP_PAD)
        ns = slice(g * SSD_STATE, (g + 1) * SSD_STATE)
        cm_g, bm_g = cmat[:, ns], bmat[:, ns]
        cb = lax.dot_general(cm_g, bm_g, (((1,), (1,)), ((), ())), preferred_element_type=F32)
        x_g = xdt_b[:, gs]
        y_g = jnp.zeros((L, GROUP_PAD), F32)
        for i in range(HEADS_PER_GROUP):
            hl = first_lane + HEADS_PER_GROUP * g + i
            diff = cs_col[:, hl:hl + 1] - cs_row[hl:hl + 1, :]
            decay = jnp.exp(jnp.where(causal, diff, NEG_BIG))
            head_lanes = jnp.logical_and(lane_g >= i * SSD_HEAD_DIM, lane_g < (i + 1) * SSD_HEAD_DIM)
            x_h = jnp.where(head_lanes, x_g, jnp.zeros((), BF16))
            y_g = y_g + jnp.dot((cb * decay).astype(BF16), x_h, preferred_element_type=F32)
        h_t = h_scr[g]
        y_in = jnp.dot(cm_g, h_t.astype(BF16), preferred_element_type=F32)
        y_g = y_g + y_in * in_decay[:, gs]
        s_new = lax.dot_general(bm_g, xdd_b[:, gs], (((0,), (0,)), ((), ())), preferred_element_type=F32)
        h_scr[g] = h_t * state_decay[:, gs] + s_new
        ys.append(y_g)
    y = jnp.concatenate(ys, axis=1)

    if rev:
        out_ref[0, rows, :] = y
    else:
        y = y + yb_ref[0, rows, :] + xs * dskip_ref[...]
        gated = y * _silu(z_ref[0, rows, :])
        outs = []
        for g in range(SSD_GROUPS):
            sl = gated[:, g * GROUP_PAD:(g + 1) * GROUP_PAD]
            ms = jnp.sum(sl * sl, axis=-1, keepdims=True) * (1.0 / GROUP_DIM)
            outs.append(sl * lax.rsqrt(ms + EPS))
        out_ref[0, rows, :] = (jnp.concatenate(outs, axis=1) * nw_ref[...]).astype(BF16)


def _ssd_tile(j, rev, n_lat_tiles, n_tiles):
    if rev:
        return n_tiles - 1 - j
    n_ctx = n_tiles - n_lat_tiles
    return jnp.where(j < n_ctx, n_lat_tiles + j, j - n_ctx)


def _ssd_scan(xs, bc, dt, z, yb, dtb, alog, dskip, nw, n_lat_chunks, rev):
    bsz, t, _ = xs.shape
    step_rows = SSD_STEP_CHUNKS * SSD_CHUNK
    n_chunks = t // step_rows
    chunk = functools.partial(_ssd_tile, rev=rev, n_lat_tiles=n_lat_chunks // SSD_STEP_CHUNKS, n_tiles=n_chunks)

    def tok(width):
        return pl.BlockSpec((1, step_rows, width), lambda b, j: (b, chunk(j), 0))

    def const(shape):
        return pl.BlockSpec(shape, lambda b, j: (0,) * len(shape))

    params = [dtb, alog]
    param_specs = [const(p.shape) for p in params]
    if rev:
        args = [xs, bc, dt] + params
        in_specs = [tok(SSD_PAD), tok(bc.shape[-1]), tok(128)] + param_specs
        out_dtype = F32
    else:
        args = [xs, bc, dt, z, yb] + params + [dskip, nw]
        in_specs = ([tok(SSD_PAD), tok(bc.shape[-1]), tok(128), tok(SSD_PAD), tok(SSD_PAD)] + param_specs
                    + [const(dskip.shape), const(nw.shape)])
        out_dtype = BF16
    return pl.pallas_call(
        functools.partial(_ssd_kernel, rev=rev),
        grid=(bsz, n_chunks),
        in_specs=in_specs,
        out_specs=tok(SSD_PAD),
        out_shape=jax.ShapeDtypeStruct((bsz, t, SSD_PAD), out_dtype),
        scratch_shapes=[pltpu.VMEM((SSD_GROUPS, SSD_STATE, GROUP_PAD), F32)],
        compiler_params=pltpu.CompilerParams(dimension_semantics=("parallel", "arbitrary"),
                                             vmem_limit_bytes=VMEM_LIMIT),
        name="ssd_bwd" if rev else "ssd_fwd",
    )(*args)


def _attn_kernel(*refs):
    q_ref, k_ref, vt_ref, kn_ref, o_ref, s_scr = refs
    tq = q_ref.shape[1]
    n_keys = k_ref.shape[1]
    chunks = [(lo, min(ATT_TK, n_keys - lo)) for lo in range(0, n_keys, ATT_TK)]
    mblk = max(8, min(ATT_TK, n_keys, (8 * 1024) // tq))

    def scores(hd, lo, size, m_acc):
        sl = slice(hd * HEAD_SLAB, (hd + 1) * HEAD_SLAB)
        s_c = lax.dot_general(k_ref[0, lo:lo + size, sl], q_ref[0, :, sl], (((1,), (1,)), ((), ())),
                              preferred_element_type=F32)
        s_scr[hd % 2, lo:lo + size, :] = s_c
        for r in range(0, size, mblk):
            blk = s_scr[hd % 2, lo + r:lo + r + mblk, :]
            m_acc = blk if m_acc is None else jnp.maximum(m_acc, blk)
        return m_acc

    def column_max(m_acc):
        rows = m_acc.shape[0]
        while rows > 8:
            rows //= 2
            m_acc = jnp.maximum(m_acc[0:rows], m_acc[rows:2 * rows])
        return jnp.max(m_acc, axis=0, keepdims=True)

    def finish(acc):
        return acc[0:V_HEAD] * (1.0 / acc[V_HEAD:V_HEAD + 1])

    def write_out(out_t):
        for pair in range(MLA_HEADS // 2):
            o_ref[0, :, pair * HEAD_SLAB:(pair + 1) * HEAD_SLAB] = (
                jnp.concatenate(out_t[2 * pair:2 * pair + 2], axis=0).T.astype(BF16))

    def bounded_path():
        k_max2 = jnp.max(kn_ref[0], axis=0)
        ones = jnp.ones((8, HEAD_SLAB), F32)
        small = [(lo, min(BOUND_TK, n_keys - lo)) for lo in range(0, n_keys, BOUND_TK)]
        slots = [(hd, lo, size) for hd in range(MLA_HEADS) for lo, size in small]

        def score_chunk(hd, lo, size):
            sl = slice(hd * HEAD_SLAB, (hd + 1) * HEAD_SLAB)
            return lax.dot_general(k_ref[0, lo:lo + size, sl], q_ref[0, :, sl], (((1,), (1,)), ((), ())),
                                   preferred_element_type=F32)

        bounds = []
        for hd in range(MLA_HEADS):
            q_f = q_ref[0, :, hd * HEAD_SLAB:(hd + 1) * HEAD_SLAB].astype(F32)
            q_n2 = lax.dot_general(ones, q_f * q_f, (((1,), (1,)), ((), ())), preferred_element_type=F32,
                                   precision=lax.Precision.HIGHEST)[0:1]
            bounds.append(jnp.sqrt(q_n2 * k_max2[hd:hd + 1, 0:1]) * BOUND_SLACK)

        out_t = []
        worst = None
        s_next = score_chunk(*slots[0])
        for idx, (hd, lo, size) in enumerate(slots):
            if lo == 0:
                bound = bounds[hd]
                m_acc = None
                acc = jnp.zeros((VT_ROWS, tq), F32)
            s_c = s_next
            if idx + 1 < len(slots):
                s_next = score_chunk(*slots[idx + 1])
            for r in range(0, size, mblk):
                blk = s_c[r:r + mblk]
                m_acc = blk if m_acc is None else jnp.maximum(m_acc, blk)
            acc = acc + jnp.dot(vt_ref[0, hd * VT_ROWS:(hd + 1) * VT_ROWS, lo:lo + size],
                                jnp.exp2(s_c - bound).astype(BF16), preferred_element_type=F32)
            if lo + size == n_keys:
                out_t.append(finish(acc))
                gap = bound - column_max(m_acc)
                worst = gap if worst is None else jnp.maximum(worst, gap)
        write_out(out_t)
        return jnp.max(worst)

    def exact_path():
        out_t = []
        m_prev = None
        for hd in range(MLA_HEADS + 1):
            m_acc = None
            acc = jnp.zeros((VT_ROWS, tq), F32)
            for lo, size in chunks:
                if hd < MLA_HEADS:
                    m_acc = scores(hd, lo, size, m_acc)
                if hd > 0:
                    p_t = jnp.exp2(s_scr[(hd - 1) % 2, lo:lo + size, :] - m_prev).astype(BF16)
                    acc = acc + jnp.dot(vt_ref[0, (hd - 1) * VT_ROWS:hd * VT_ROWS, lo:lo + size], p_t,
                                        preferred_element_type=F32)
            if hd > 0:
                out_t.append(finish(acc))
            if hd < MLA_HEADS:
                m_prev = column_max(m_acc)
        write_out(out_t)

    worst_gap = bounded_path()

    @pl.when(jnp.logical_not(worst_gap <= MAX_BOUND_GAP))
    def _():
        exact_path()


def _attention(q, k, vt, kn, seq, ctx, latent):
    bsz, t, _ = q.shape
    n_kn = kn.shape[1]
    assert ctx == TM
    tq = ATT_TQ_LATENT if latent else ATT_TQ
    n_tiles = (seq if latent else ctx) // tq
    q_off = 0 if latent else seq // tq
    n_keys = t if latent else ctx
    key_blk = 0 if latent else seq // ctx
    nh = MLA_HEADS * HEAD_SLAB
    in_specs = [pl.BlockSpec((1, tq, nh), lambda b, i: (b, q_off + i, 0)),
                pl.BlockSpec((1, n_keys, nh), lambda b, i: (b, key_blk, 0)),
                pl.BlockSpec((1, MLA_HEADS * VT_ROWS, n_keys), lambda b, i: (b, 0, key_blk)),
                pl.BlockSpec((1, n_kn, 8, HEAD_SLAB), lambda b, i: (b, 0, 0, 0)) if latent else
                pl.BlockSpec((1, 1, 8, HEAD_SLAB), lambda b, i: (b, n_kn - 1, 0, 0))]
    return pl.pallas_call(
        _attn_kernel,
        grid=(bsz, n_tiles),
        in_specs=in_specs,
        out_specs=pl.BlockSpec((1, tq, MLA_OUT), lambda b, i: (b, i, 0)),
        out_shape=jax.ShapeDtypeStruct((bsz, n_tiles * tq, MLA_OUT), BF16),
        scratch_shapes=[pltpu.VMEM((2, n_keys, tq), F32)],
        compiler_params=pltpu.CompilerParams(dimension_semantics=("parallel", "arbitrary"),
                                             vmem_limit_bytes=VMEM_LIMIT),
        name="mla_attention" if latent else "mla_attention_ctx",
    )(q, k, vt, kn)


def _pool_mix(pool_ref, pprev_ref, pnext_ref, ext_scr, pw_ref, ps_ref, tile, n_lat_tiles, n_tiles, seq, ctx):
    seg_first = jnp.logical_or(tile == 0, tile == n_lat_tiles)
    seg_last = jnp.logical_or(tile == n_lat_tiles - 1, tile == n_tiles - 1)
    h0 = POOL_HALO
    ext_scr[0:h0, :] = pprev_ref[0] * jnp.where(seg_first, 0.0, 1.0)
    ext_scr[h0:h0 + TM, :] = pool_ref[0]
    ext_scr[h0 + TM:h0 + TM + h0, :] = pnext_ref[0] * jnp.where(seg_last, 0.0, 1.0)

    is_ctx = tile >= n_lat_tiles
    seg_len = jnp.where(is_ctx, ctx, seq)
    pos = lax.broadcasted_iota(jnp.int32, (TM, 1), 0) + jnp.where(is_ctx, tile - n_lat_tiles, tile) * TM
    lane = lax.broadcasted_iota(jnp.int32, (1, 128), 1)
    low_half = lane < POOL_GROUP_DIM

    def inv_count(w):
        cnt = jnp.minimum(pos + (w - w // 2), seg_len) - jnp.maximum(pos - w // 2, 0)
        return 1.0 / cnt.astype(F32)

    def taps(slab, offsets):
        tot = None
        for k in offsets:
            piece = ext_scr[h0 + k:h0 + k + TM, slab * 128:(slab + 1) * 128]
            tot = piece if tot is None else tot + piece
        return tot

    outs = []
    for slab in range(2):
        w_small, w_big = POOL_WINDOWS[2 * slab], POOL_WINDOWS[2 * slab + 1]
        small = taps(slab, range(-(w_small // 2), w_small - w_small // 2))
        extra = [k for k in range(-(w_big // 2), w_big - w_big // 2)
                 if not -(w_small // 2) <= k < w_small - w_small // 2]
        big = small + taps(slab, extra)
        mean = jnp.where(low_half, small * inv_count(w_small), big * inv_count(w_big))
        outs.append(mean - ext_scr[h0:h0 + TM, slab * 128:(slab + 1) * 128])
    d = jnp.concatenate(outs, axis=1).astype(BF16)
    return jnp.dot(d, pw_ref[...], preferred_element_type=F32) * ps_ref[...]


def _mix_mlp_kernel(*refs, final, n_lat_tiles, n_tiles, seq, ctx, split_ctx, split_attn):
    x_ref, ctx_ref = (refs[0], refs[1]) if split_ctx else (refs[0], None)
    refs = refs[2 if split_ctx else 1:]
    ssd_ref, attn_ref = refs[0], refs[1]
    attn_ctx_ref = refs[2] if split_attn else None
    refs = refs[3 if split_attn else 2:]
    if final:
        (pool_ref, pprev_ref, pnext_ref, mod_ref, wos_ref, woa_ref, wop_ref,
         pw_ref, ps_ref, n2_ref, w1_ref, w2_ref, fn_ref, o_ref, ext_scr) = refs
    else:
        (pool_ref, pprev_ref, pnext_ref, mod_ref, wos_ref, woa_ref, wop_ref,
         pw_ref, ps_ref, n2_ref, w1_ref, w2_ref, o_ref, ext_scr) = refs
    tile = pl.program_id(1)
    pool_y = _pool_mix(pool_ref, pprev_ref, pnext_ref, ext_scr, pw_ref, ps_ref, tile,
                       n_lat_tiles, n_tiles, seq, ctx)
    mix = (jnp.dot(ssd_ref[0], wos_ref[...], preferred_element_type=F32)
           + jnp.dot(_token_tile(attn_ref, attn_ctx_ref, tile, n_lat_tiles), woa_ref[...],
                     preferred_element_type=F32)
           + jnp.dot(pool_y.astype(BF16), wop_ref[...], preferred_element_type=F32))
    x1 = _token_tile(x_ref, ctx_ref, tile, n_lat_tiles) + mod_ref[0, 2:3, :] * mix
    h = _rms(x1, n2_ref[...])
    hb = (h * (1.0 + mod_ref[0, 4:5, :]) + mod_ref[0, 3:4, :]).astype(BF16)
    acc = jnp.zeros((TM, D_MODEL), F32)
    for lo in range(0, D_FF, FF_CHUNK):
        a = jnp.maximum(jnp.dot(hb, w1_ref[:, lo:lo + FF_CHUNK], preferred_element_type=F32), 0.0)
        acc = acc + jnp.dot((a * a).astype(BF16), w2_ref[lo:lo + FF_CHUNK, :], preferred_element_type=F32)
    x2 = x1 + mod_ref[0, 5:6, :] * acc
    if final:
        x2 = _rms(x2, fn_ref[...])
    o_ref[0] = x2


def _mix_mlp(xc, ctx_in, ssd, attn, attn_ctx, pool, mod, wos, woa, wop, pw, ps, n2, w1, w2, fn, n_lat_tiles,
             seq, ctx, final):
    bsz, t, _ = ssd.shape
    split_ctx = ctx_in is not None
    split_attn = attn_ctx is not None
    assert final or split_attn
    n_tiles = t // TM
    out_tiles = n_lat_tiles if final else n_tiles
    halo_blocks = t // POOL_HALO
    per = TM // POOL_HALO
    ctx_row = mod.shape[0] - 1
    last_x_tile = xc.shape[1] // TM - 1

    def tok(width):
        return pl.BlockSpec((1, TM, width), lambda b, i: (b, i, 0))

    x_specs = [pl.BlockSpec((1, TM, D_MODEL), lambda b, i: (b, jnp.minimum(i, last_x_tile), 0))]
    x_args = [xc]
    if split_ctx:
        x_specs.append(pl.BlockSpec((1, TM, D_MODEL), lambda b, i: (b, 0, 0)))
        x_args.append(ctx_in)

    attn_specs = [pl.BlockSpec((1, TM, MLA_OUT), lambda b, i: (b, jnp.minimum(i, n_lat_tiles - 1), 0))]
    attn_args = [attn]
    if split_attn:
        attn_specs.append(pl.BlockSpec((1, TM, MLA_OUT), lambda b, i: (b, 0, 0)))
        attn_args.append(attn_ctx)

    def const(shape):
        return pl.BlockSpec(shape, lambda b, i: (0,) * len(shape), pipeline_mode=pl.Buffered(1))

    prev_spec = pl.BlockSpec((1, POOL_HALO, POOL_DIM), lambda b, i: (b, jnp.maximum(i * per - 1, 0), 0))
    next_spec = pl.BlockSpec((1, POOL_HALO, POOL_DIM),
                             lambda b, i: (b, jnp.minimum((i + 1) * per, halo_blocks - 1), 0))
    mod_spec = pl.BlockSpec((1, 6, D_MODEL), lambda b, i: (jnp.where(i >= n_lat_tiles, ctx_row, b), 0, 0))
    weights = [wos, woa, wop, pw, ps, n2, w1, w2] + ([fn] if final else [])
    return pl.pallas_call(
        functools.partial(_mix_mlp_kernel, final=final, n_lat_tiles=n_lat_tiles, n_tiles=n_tiles,
                          seq=seq, ctx=ctx, split_ctx=split_ctx, split_attn=split_attn),
        grid=(bsz, out_tiles),
        in_specs=x_specs + [tok(SSD_PAD)] + attn_specs + [tok(POOL_DIM), prev_spec, next_spec, mod_spec]
                 + [const(w.shape) for w in weights],
        out_specs=tok(D_MODEL),
        out_shape=jax.ShapeDtypeStruct((bsz, out_tiles * TM, D_MODEL), F32),
        scratch_shapes=[pltpu.VMEM((TM + 2 * POOL_HALO, POOL_DIM), F32)],
        compiler_params=pltpu.CompilerParams(dimension_semantics=("parallel", "parallel"),
                                             vmem_limit_bytes=VMEM_LIMIT),
        name="mix_mlp_final" if final else "mix_mlp",
    )(*x_args, ssd, *attn_args, pool, pool, pool, mod, *weights)


def kernel(x, c, ctx, c_ctx, mod_w, mod_b, norm1_w, norm2_w, w_in, conv_w, conv_b, dt_bias, a_log, ssd_d,
           ssd_norm_w, q_a_norm_w, w_q_b, kv_a_norm_w, w_kv_b, pool_w, pool_scale, w_out, w_mlp1, w_mlp2,
           final_norm_w):
    bsz, seq, _ = x.shape
    n_ctx = ctx.shape[1]
    depth = mod_w.shape[0]
    assert seq % TM == 0 and n_ctx % TM == 0 and seq % GRID_W == 0
    assert seq % (SSD_STEP_CHUNKS * SSD_CHUNK) == 0 and n_ctx % (SSD_STEP_CHUNKS * SSD_CHUNK) == 0
    assert seq % ATT_TQ_LATENT == 0 and n_ctx % ATT_TQ == 0 and seq % n_ctx == 0
    n_lat_tiles = seq // TM
    n_lat_chunks = seq // SSD_CHUNK

    assert n_ctx == TM
    xc, ctx_in = x, ctx
    cond_rows = -(-(bsz + 1) // 8) * 8
    cond = jnp.zeros((cond_rows, D_MODEL), F32).at[:bsz].set(c).at[bsz].set(c_ctx)
    mod_all = _modulation(cond, mod_w, mod_b)[:, :bsz + 1]
    rope = _rope_tables(seq, n_ctx)

    win_idx, wq_idx, wkv_idx = _win_index(), _wq_index(), _wkv_index()
    pad_idx = _pad_group_index(0)
    xbc_idx = np.concatenate([pad_idx, SSD_INNER + np.arange(XBC_DIM - SSD_INNER)])
    row = lambda v: v.reshape(1, -1)

    for i in range(depth):
        final = i == depth - 1
        win = _remap_cols(w_in[i], win_idx).astype(BF16)
        wq = _remap_cols(w_q_b[i], wq_idx).astype(BF16)
        wkv = _remap_cols(w_kv_b[i], wkv_idx).astype(BF16)
        convw = _remap_cols(conv_w[i], xbc_idx)
        convb = row(_remap_cols(conv_b[i], xbc_idx))
        dtb = row(jnp.pad(dt_bias[i].reshape(-1), (0, 128 - 2 * SSD_HEADS)))
        alog = row(jnp.pad(a_log[i].reshape(-1), (0, 128 - 2 * SSD_HEADS)))
        dskip = row(_remap_cols(jnp.repeat(ssd_d[i], SSD_HEAD_DIM), pad_idx))
        nw = row(_remap_cols(ssd_norm_w[i], pad_idx))
        wo = w_out[i].astype(BF16)
        wos = _remap_cols(wo[:SSD_INNER].T, pad_idx).T
        woa = wo[SSD_INNER:SSD_INNER + MLA_OUT]
        wop = wo[SSD_INNER + MLA_OUT:]
        pw = jax.scipy.linalg.block_diag(*[pool_w[i, g] for g in range(len(POOL_WINDOWS))]).astype(BF16)

        z, xs, bc, dt, pool, q, k, v, kn = _inproj(xc, ctx_in, mod_all[i], row(norm1_w[i]), win, convw, convb,
                                                    row(q_a_norm_w[i]), row(kv_a_norm_w[i]), wq, wkv, rope,
                                                    n_lat_tiles)
        yb = _ssd_scan(xs, bc, dt, None, None, dtb, alog, None, None, n_lat_chunks, rev=True)
        ssd = _ssd_scan(xs, bc, dt, z, yb, dtb, alog, dskip, nw, n_lat_chunks, rev=False)
        attn = _attention(q, k, v, kn, seq, n_ctx, latent=True)
        attn_ctx = None if final else _attention(q, k, v, kn, seq, n_ctx, latent=False)
        xc = _mix_mlp(xc, ctx_in, ssd, attn, attn_ctx, pool, mod_all[i], wos, woa, wop, pw, row(pool_scale[i]),
                      row(norm2_w[i]), w_mlp1[i].astype(BF16), w_mlp2[i].astype(BF16), row(final_norm_w),
                      n_lat_tiles, seq, n_ctx, final)
        ctx_in = None
    return xc
```
